```python
import jax, jax.numpy as jnp
from jax import lax
import numpy as np

D_MODEL = 1024
BATCH = 2
SEQ = 16384
DEPTH = 1
DEC_BATCH = 1
DEC_SEQ = 16384
PAST_LEN = 128

GRID_W = 64
ROPE_THETA = 10000.0
QBLK = 128
EPS = 1e-6
H_A = 8
G_A = 2
HD_A = 64
H_B = 8
Q_LORA = 256
KV_LORA = 128
NOPE_B = 64
ROPE_B = 32
V_B = 64
COLS_QA = H_A * HD_A
COLS_KA = G_A * HD_A
COLS_VA = G_A * HD_A
COLS_CQ = Q_LORA
COLS_CKV = KV_LORA
COLS_KR = ROPE_B
COLS_GA = D_MODEL
COLS_GB = D_MODEL
IN_COLS = COLS_QA + COLS_KA + COLS_VA + COLS_CQ + COLS_CKV + COLS_KR + COLS_GA + COLS_GB
PEER_HEADS = 8
N_KEYS = 128
N_EXPERTS = N_KEYS * N_KEYS
D_KEY = 256
PEER_TOPK = 16
PEER_BLK = 128

kernel_name = "hybrid_gqa_mla_peer_encoder"


def rmsnorm(x, g):
    xf = x.astype(jnp.float32)
    xf = xf * lax.rsqrt(jnp.mean(xf * xf, axis=-1, keepdims=True) + EPS)
    return xf.astype(x.dtype) * g


def axial_angles(S, half_dim):
    rows = S // GRID_W
    r = jnp.repeat(jnp.arange(rows, dtype=jnp.float32), GRID_W)
    c = jnp.tile(jnp.arange(GRID_W, dtype=jnp.float32), rows)
    freqs = ROPE_THETA ** (-jnp.arange(0, half_dim, 2, dtype=jnp.float32) / half_dim)
    return r[:, None] * freqs, c[:, None] * freqs


def rope_half(x, ang):
    cos = jnp.cos(ang)[None, :, None, :].astype(x.dtype)
    sin = jnp.sin(ang)[None, :, None, :].astype(x.dtype)
    x1, x2 = jnp.split(x, 2, axis=-1)
    return jnp.concatenate([x1 * cos - x2 * sin, x1 * sin + x2 * cos], axis=-1)


def rope_axial(x):
    S, d = x.shape[1], x.shape[-1]
    half = d // 2
    ang_r, ang_c = axial_angles(S, half)
    return jnp.concatenate([rope_half(x[..., :half], ang_r), rope_half(x[..., half:], ang_c)], axis=-1)


def block_attention(q, k, v):
    B, S, H, dq = q.shape
    G, dv = k.shape[2], v.shape[-1]
    rep = H // G
    scale = dq ** -0.5
    qb = q.reshape(B, S // QBLK, QBLK, G, rep, dq).transpose(1, 0, 2, 3, 4, 5)

    def one_block(qblk):
        s = jnp.einsum('bqgrd,bkgd->bgrqk', qblk, k).astype(jnp.float32) * scale
        p = jax.nn.softmax(s, axis=-1).astype(v.dtype)
        return jnp.einsum('bgrqk,bkgd->bqgrd', p, v)

    o = lax.map(one_block, qb)
    return o.transpose(1, 0, 2, 3, 4, 5).reshape(B, S, H, dv)


def peer(xn, w_pq, sub_keys, expert_u, expert_v):
    B, S, D = xn.shape
    T = B * S
    xc = xn.reshape(T // PEER_BLK, PEER_BLK, D)

    def one_block(xb):
        blk = xb.shape[0]
        q = (xb @ w_pq).reshape(blk, PEER_HEADS, 2, D_KEY // 2)
        s = jnp.einsum('thpd,hpnd->thpn', q, sub_keys).astype(jnp.float32)
        sv, si = lax.top_k(s, PEER_TOPK)
        cand = (sv[:, :, 0, :, None] + sv[:, :, 1, None, :]).reshape(blk, PEER_HEADS, PEER_TOPK * PEER_TOPK)
        cidx = (si[:, :, 0, :, None] * N_KEYS + si[:, :, 1, None, :]).reshape(blk, PEER_HEADS, PEER_TOPK * PEER_TOPK)
        cv, ci = lax.top_k(cand, PEER_TOPK)
        e = jnp.take_along_axis(cidx, ci, axis=-1)
        g = jax.nn.softmax(cv, axis=-1).astype(xb.dtype)
        ue = jnp.take(expert_u, e, axis=0)
        hid = jax.nn.gelu(jnp.einsum('td,thkd->thk', xb, ue), approximate=False)
        ve = jnp.take(expert_v, e, axis=0)
        return jnp.einsum('thk,thkd->td', g * hid, ve)

    return lax.map(one_block, xc).reshape(B, S, D)


def token_mixers(xn, w_in, qa_norm_g, ka_norm_g, cq_norm_g, ckv_norm_g, w_uq, w_ukv, w_br_a, w_br_b, w_out):
    B, S, _ = xn.shape
    proj = xn @ w_in
    splits = np.cumsum([COLS_QA, COLS_KA, COLS_VA, COLS_CQ, COLS_CKV, COLS_KR, COLS_GA]).tolist()
    qa, ka, va, cq, ckv, kr, ga, gb = jnp.split(proj, splits, axis=-1)
    qa = rope_axial(rmsnorm(qa.reshape(B, S, H_A, HD_A), qa_norm_g))
    ka = rope_axial(rmsnorm(ka.reshape(B, S, G_A, HD_A), ka_norm_g))
    va = va.reshape(B, S, G_A, HD_A)
    oa = block_attention(qa, ka, va).reshape(B, S, H_A * HD_A)
    qb = (rmsnorm(cq, cq_norm_g) @ w_uq).reshape(B, S, H_B, NOPE_B + ROPE_B)
    qb = jnp.concatenate([qb[..., :NOPE_B], rope_axial(qb[..., NOPE_B:])], axis=-1)
    kvb = (rmsnorm(ckv, ckv_norm_g) @ w_ukv).reshape(B, S, H_B, NOPE_B + V_B)
    k_nope, vb = kvb[..., :NOPE_B], kvb[..., NOPE_B:]
    k_rope = rope_axial(kr.reshape(B, S, 1, ROPE_B))
    kb = jnp.concatenate([k_nope, jnp.broadcast_to(k_rope, (B, S, H_B, ROPE_B))], axis=-1)
    ob = block_attention(qb, kb, vb).reshape(B, S, H_B * V_B)
    merged = jax.nn.sigmoid(ga) * (oa @ w_br_a) + jax.nn.sigmoid(gb) * (ob @ w_br_b)
    return merged @ w_out


def trunk(x, ln1_g, w_in, qa_norm_g, ka_norm_g, cq_norm_g, ckv_norm_g, w_uq, w_ukv,
          w_br_a, w_br_b, w_out, ln2_g, w_pq, sub_keys, expert_u, expert_v, final_g):
    h = x
    for l in range(DEPTH):
        h = h + token_mixers(rmsnorm(h, ln1_g[l]), w_in[l], qa_norm_g[l], ka_norm_g[l], cq_norm_g[l],
                             ckv_norm_g[l], w_uq[l], w_ukv[l], w_br_a[l], w_br_b[l], w_out[l])
        h = h + peer(rmsnorm(h, ln2_g[l]), w_pq[l], sub_keys[l], expert_u[l], expert_v[l])
    return rmsnorm(h, final_g)


def setup_inputs(seed: int = 0) -> dict:
    key = jax.random.key(seed)
    ks = jax.random.split(key, 20)
    f32 = jnp.float32

    def nrm(k, shape, scale):
        return jax.random.normal(k, shape, f32) * scale

    def gain(k, shape):
        return 1.0 + 0.01 * jax.random.normal(k, shape, f32)

    return {
        "x_prompt": nrm(ks[0], (BATCH, SEQ, D_MODEL), 1.0),
        "x_sample": nrm(ks[1], (DEC_BATCH, DEC_SEQ, D_MODEL), 1.0),
        "ln1_g": gain(ks[2], (DEPTH, D_MODEL)),
        "w_in": nrm(ks[3], (DEPTH, D_MODEL, IN_COLS), D_MODEL ** -0.5),
        "qa_norm_g": gain(ks[4], (DEPTH, HD_A)),
        "ka_norm_g": gain(ks[5], (DEPTH, HD_A)),
        "cq_norm_g": gain(ks[6], (DEPTH, Q_LORA)),
        "ckv_norm_g": gain(ks[7], (DEPTH, KV_LORA)),
        "w_uq": nrm(ks[8], (DEPTH, Q_LORA, H_B * (NOPE_B + ROPE_B)), Q_LORA ** -0.5),
        "w_ukv": nrm(ks[9], (DEPTH, KV_LORA, H_B * (NOPE_B + V_B)), KV_LORA ** -0.5),
        "w_br_a": nrm(ks[10], (DEPTH, H_A * HD_A, D_MODEL), (H_A * HD_A) ** -0.5),
        "w_br_b": nrm(ks[11], (DEPTH, H_B * V_B, D_MODEL), (H_B * V_B) ** -0.5),
        "w_out": nrm(ks[12], (DEPTH, D_MODEL, D_MODEL), D_MODEL ** -0.5),
        "ln2_g": gain(ks[13], (DEPTH, D_MODEL)),
        "w_pq": nrm(ks[14], (DEPTH, D_MODEL, PEER_HEADS * D_KEY), D_MODEL ** -0.5),
        "sub_keys": nrm(ks[15], (DEPTH, PEER_HEADS, 2, N_KEYS, D_KEY // 2), (D_KEY // 2) ** -0.5),
        "expert_u": nrm(ks[16], (DEPTH, N_EXPERTS, D_MODEL), D_MODEL ** -0.5),
        "expert_v": nrm(ks[17], (DEPTH, N_EXPERTS, D_MODEL), PEER_HEADS ** -0.5),
        "final_g": gain(ks[18], (D_MODEL,)),
    }


def reference(x_prompt, x_sample, ln1_g, w_in, qa_norm_g, ka_norm_g, cq_norm_g, ckv_norm_g, w_uq, w_ukv,
              w_br_a, w_br_b, w_out, ln2_g, w_pq, sub_keys, expert_u, expert_v, final_g):
    y_prompt = trunk(x_prompt, ln1_g, w_in, qa_norm_g, ka_norm_g, cq_norm_g, ckv_norm_g, w_uq, w_ukv,
                     w_br_a, w_br_b, w_out, ln2_g, w_pq, sub_keys, expert_u, expert_v, final_g)
    y_sample = trunk(x_sample, ln1_g, w_in, qa_norm_g, ka_norm_g, cq_norm_g, ckv_norm_g, w_uq, w_ukv,
                     w_br_a, w_br_b, w_out, ln2_g, w_pq, sub_keys, expert_u, expert_v, final_g)
    return (y_prompt, y_sample)
```

```python
import functools
import math

import numpy as np
import jax
import jax.numpy as jnp
from jax import lax
from jax.experimental import pallas as pl
from jax.experimental.pallas import tpu as pltpu

D_MODEL = 1024
GRID_W = 64
ROPE_THETA = 10000.0
EPS = 1e-6
H_A, G_A, HD_A = 8, 2, 64
H_B, Q_LORA, KV_LORA, NOPE_B, ROPE_B, V_B = 8, 256, 128, 64, 32, 64
PEER_HEADS, N_KEYS, D_KEY, PEER_TOPK = 8, 128, 256, 16
N_EXPERTS = N_KEYS * N_KEYS
HALF_KEY = D_KEY // 2

LANES = 128
SUBLANES = 8
VMEM_LIMIT_BYTES = 56 * 1024 * 1024

DQ_PAD = LANES
LOG2E = math.log2(math.e)
MM_DTYPE = jnp.bfloat16

TM_PROJ = 512
TQ_ATTN = 512
TK_ATTN = 256
TM_POST = 512
TR_ROUTE = SUBLANES * LANES
TT_PEER = 512
EC_PEER = 512

_R_QA, _R_QA2, _R_KA, _R_KA2, _R_VA, _R_CQ, _R_CKV, _R_KR, _R_KR2, _R_END = (
    0, 512, 1024, 1152, 1280, 1408, 1664, 1792, 1824, 1856)
_T_QAC, _T_QAS, _T_KAC, _T_KAS, _T_QBC, _T_QBS, _T_KBC, _T_KBS, _T_END = (
    0, 64, 128, 192, 256, 288, 320, 352, 384)


def _cparams(sem):
    return pltpu.CompilerParams(dimension_semantics=sem, vmem_limit_bytes=VMEM_LIMIT_BYTES)


def _dot(a, b):
    return jnp.dot(a, b, preferred_element_type=jnp.float32)


def _rope_partner(d):
    half, qtr = d // 2, d // 4
    e = np.arange(d)
    within = e % half
    first = within < qtr
    partner = np.where(first, e + qtr, e - qtr)
    sign = np.where(first, -1.0, 1.0).astype(np.float32)
    freq_idx = within % qtr
    use_col = e >= half
    return partner, sign, freq_idx, use_col


def _rope_tables(S, d):
    half = d // 2
    partner, sign, freq_idx, use_col = _rope_partner(d)
    t = jnp.arange(S, dtype=jnp.int32)
    r = (t // GRID_W).astype(jnp.float32)
    c = (t % GRID_W).astype(jnp.float32)
    freqs = ROPE_THETA ** (-jnp.arange(0, half, 2, dtype=jnp.float32) / half)
    f = freqs[freq_idx]
    pos = jnp.where(jnp.asarray(use_col)[:, None], c[None, :], r[None, :])
    ang = pos * f[:, None]
    return jnp.cos(ang), jnp.sin(ang) * jnp.asarray(sign)[:, None], partner


def _prepare(S, ln1_g, w_in, qa_norm_g, ka_norm_g, cq_norm_g, ckv_norm_g, w_uq, w_ukv,
             w_br_a, w_br_b, w_out, ln2_g, w_pq, sub_keys, expert_u, expert_v, final_g):
    bf = MM_DTYPE
    w = w_in[0]
    c_qa, c_ka, c_va, c_cq, c_ckv, c_kr, c_ga = np.cumsum([512, 128, 128, 256, 128, 32, 1024]).tolist()
    p64, _, _, _ = _rope_partner(HD_A)
    p32, _, _, _ = _rope_partner(ROPE_B)
    qa_perm = (np.arange(H_A)[:, None] * HD_A + p64[None, :]).reshape(-1)
    ka_perm = (np.arange(G_A)[:, None] * HD_A + p64[None, :]).reshape(-1)
    w_qa, w_ka = w[:, :c_qa], w[:, c_qa:c_ka]
    w_kr = w[:, c_ckv:c_kr]
    w1 = jnp.concatenate([
        w_qa, w_qa[:, qa_perm], w_ka, w_ka[:, ka_perm], w[:, c_ka:c_va],
        w[:, c_va:c_cq], w[:, c_cq:c_ckv], w_kr, w_kr[:, p32]], axis=1)
    w1t = w1.T.astype(bf)
    wgt = w[:, c_kr:].T.astype(bf)

    uq = w_uq[0].reshape(Q_LORA, H_B, NOPE_B + ROPE_B)
    uq_nope = uq[:, :, :NOPE_B].reshape(Q_LORA, H_B * NOPE_B)
    uq_rope = uq[:, :, NOPE_B:]
    wuqt = jnp.concatenate([uq_nope, uq_rope.reshape(Q_LORA, -1),
                            uq_rope[:, :, p32].reshape(Q_LORA, -1)], axis=1).T.astype(bf)
    ukv = w_ukv[0].reshape(KV_LORA, H_B, NOPE_B + V_B)
    wukvt = jnp.concatenate([ukv[:, :, :NOPE_B].reshape(KV_LORA, -1),
                             ukv[:, :, NOPE_B:].reshape(KV_LORA, -1)], axis=1).T.astype(bf)

    cos64, sin64, _ = _rope_tables(S, HD_A)
    cos32, sin32, _ = _rope_tables(S, ROPE_B)
    sa = (HD_A ** -0.5) * LOG2E
    sb = ((NOPE_B + ROPE_B) ** -0.5) * LOG2E
    gq, gk = qa_norm_g[0], ka_norm_g[0]
    tabs = jnp.concatenate([
        cos64 * (gq * sa)[:, None], sin64 * (gq[p64] * sa)[:, None],
        cos64 * gk[:, None], sin64 * gk[p64][:, None],
        cos32 * sb, sin32 * sb, cos32, sin32], axis=0)

    def bcast(g, n):
        return jnp.broadcast_to(g.reshape(-1, 1), (g.size, n))

    return dict(
        ln1=ln1_g[0].reshape(1, D_MODEL), w1t=w1t, wgt=wgt, wuqt=wuqt, wukvt=wukvt, tabs=tabs,
        gcq=bcast(cq_norm_g[0], TM_PROJ), gckv=bcast(ckv_norm_g[0], TM_PROJ),
        wbrat=w_br_a[0].T.astype(bf), wbrbt=w_br_b[0].T.astype(bf), woutt=w_out[0].T.astype(bf),
        g2=bcast(ln2_g[0], TM_POST), wpqt=w_pq[0].T.astype(bf),
        subk=sub_keys[0].reshape(PEER_HEADS * 2, N_KEYS, HALF_KEY).astype(bf),
        u=expert_u[0].astype(bf), vt=expert_v[0].T.astype(bf),
        gfin=bcast(final_g, TT_PEER), sb=sb)


def _proj_kernel(x_ref, ln1_ref, w1t_ref, tab_ref, gcq_ref, gckv_ref, wuqt_ref, wukvt_ref,
                 qat_ref, ka_ref, vat_ref, qbt_ref, kb_ref, vbt_ref, *, sb):
    bf = MM_DTYPE
    x = x_ref[0]
    tm = x.shape[0]
    xn = x * lax.rsqrt(jnp.mean(x * x, axis=-1, keepdims=True) + EPS) * ln1_ref[...]
    xnt = xn.T.astype(bf)
    p = _dot(w1t_ref[...], xnt)
    tab = tab_ref[...]
    zeros64 = jnp.zeros((64, tm), jnp.float32)
    zeros32 = jnp.zeros((32, tm), jnp.float32)

    def rstd(y):
        return lax.rsqrt(jnp.mean(y * y, axis=0, keepdims=True) + EPS)

    for h in range(H_A):
        y = p[_R_QA + h * HD_A:_R_QA + (h + 1) * HD_A]
        y2 = p[_R_QA2 + h * HD_A:_R_QA2 + (h + 1) * HD_A]
        q = (y * tab[_T_QAC:_T_QAS] + y2 * tab[_T_QAS:_T_KAC]) * rstd(y)
        qat_ref[0, h * DQ_PAD:h * DQ_PAD + HD_A, :] = q.astype(bf)
        qat_ref[0, h * DQ_PAD + HD_A:(h + 1) * DQ_PAD, :] = zeros64.astype(bf)
    for g in range(G_A):
        y = p[_R_KA + g * HD_A:_R_KA + (g + 1) * HD_A]
        y2 = p[_R_KA2 + g * HD_A:_R_KA2 + (g + 1) * HD_A]
        k = (y * tab[_T_KAC:_T_KAS] + y2 * tab[_T_KAS:_T_QBC]) * rstd(y)
        ka_ref[0, g] = jnp.concatenate([k, zeros64], axis=0).T.astype(bf)
    vat_ref[0] = p[_R_VA:_R_CQ].astype(bf)

    cq = p[_R_CQ:_R_CKV]
    cqn = (cq * rstd(cq) * gcq_ref[...]).astype(bf)
    q2 = _dot(wuqt_ref[...], cqn)
    for h in range(H_B):
        nope = q2[h * NOPE_B:(h + 1) * NOPE_B] * sb
        yr = q2[512 + h * ROPE_B:512 + (h + 1) * ROPE_B]
        yr2 = q2[768 + h * ROPE_B:768 + (h + 1) * ROPE_B]
        rope = yr * tab[_T_QBC:_T_QBS] + yr2 * tab[_T_QBS:_T_KBC]
        base = h * DQ_PAD
        qbt_ref[0, base:base + NOPE_B, :] = nope.astype(bf)
        qbt_ref[0, base + NOPE_B:base + NOPE_B + ROPE_B, :] = rope.astype(bf)
        qbt_ref[0, base + NOPE_B + ROPE_B:base + DQ_PAD, :] = zeros32.astype(bf)
    ckv = p[_R_CKV:_R_KR]
    ckvn = (ckv * rstd(ckv) * gckv_ref[...]).astype(bf)
    kv = _dot(wukvt_ref[...], ckvn)
    krope = p[_R_KR:_R_KR2] * tab[_T_KBC:_T_KBS] + p[_R_KR2:_R_END] * tab[_T_KBS:_T_END]
    for h in range(H_B):
        kt = jnp.concatenate([kv[h * NOPE_B:(h + 1) * NOPE_B], krope, zeros32], axis=0)
        kb_ref[0, h] = kt.T.astype(bf)
    vbt_ref[0] = kv[H_B * NOPE_B:].astype(bf)


def _proj(x, W):
    B, S, _ = x.shape
    tm = TM_PROJ
    bf = MM_DTYPE
    const = lambda shape: pl.BlockSpec(shape, lambda b, i: (0,) * len(shape))
    return pl.pallas_call(
        functools.partial(_proj_kernel, sb=W["sb"]),
        grid=(B, S // tm),
        in_specs=[
            pl.BlockSpec((1, tm, D_MODEL), lambda b, i: (b, i, 0)),
            const((1, D_MODEL)), const((_R_END, D_MODEL)),
            pl.BlockSpec((_T_END, tm), lambda b, i: (0, i)),
            const((Q_LORA, tm)), const((KV_LORA, tm)),
            const((1024, Q_LORA)), const((1024, KV_LORA)),
        ],
        out_specs=[
            pl.BlockSpec((1, H_A * DQ_PAD, tm), lambda b, i: (b, 0, i)),
            pl.BlockSpec((1, G_A, tm, DQ_PAD), lambda b, i: (b, 0, i, 0)),
            pl.BlockSpec((1, G_A * HD_A, tm), lambda b, i: (b, 0, i)),
            pl.BlockSpec((1, H_B * DQ_PAD, tm), lambda b, i: (b, 0, i)),
            pl.BlockSpec((1, H_B, tm, DQ_PAD), lambda b, i: (b, 0, i, 0)),
            pl.BlockSpec((1, H_B * V_B, tm), lambda b, i: (b, 0, i)),
        ],
        out_shape=[
            jax.ShapeDtypeStruct((B, H_A * DQ_PAD, S), bf),
            jax.ShapeDtypeStruct((B, G_A, S, DQ_PAD), bf),
            jax.ShapeDtypeStruct((B, G_A * HD_A, S), bf),
            jax.ShapeDtypeStruct((B, H_B * DQ_PAD, S), bf),
            jax.ShapeDtypeStruct((B, H_B, S, DQ_PAD), bf),
            jax.ShapeDtypeStruct((B, H_B * V_B, S), bf),
        ],
        compiler_params=_cparams(("parallel", "parallel")),
        name="proj",
    )(x, W["ln1"], W["w1t"], W["tabs"], W["gcq"], W["gckv"], W["wuqt"], W["wukvt"])


def _attn_kernel(qt_ref, k_ref, vt_ref, ot_ref, *, heads, rep, dv, tk, heads_per_loop):
    S = k_ref.shape[2]
    tq = qt_ref.shape[2]
    nk = S // tk
    for h0 in range(0, heads, heads_per_loop):
        hs = list(range(h0, h0 + heads_per_loop))

        def body(c, carry, hs=hs):
            off = pl.multiple_of(c * tk, tk)
            out = []
            for (m, l, acc), h in zip(carry, hs):
                g = h // rep
                kc = k_ref[0, g, pl.ds(off, tk), :]
                s = _dot(kc, qt_ref[0, h * DQ_PAD:(h + 1) * DQ_PAD, :])
                mn = jnp.maximum(m, jnp.max(s, axis=0, keepdims=True))
                alpha = jnp.exp2(m - mn)
                pexp = jnp.exp2(s - mn)
                l = alpha * l + jnp.sum(pexp, axis=0, keepdims=True)
                vc = vt_ref[0, g * dv:(g + 1) * dv, pl.ds(off, tk)]
                acc = alpha * acc + _dot(vc, pexp.astype(MM_DTYPE))
                out.append((mn, l, acc))
            return tuple(out)

        init = tuple((jnp.full((1, tq), -jnp.inf, jnp.float32), jnp.zeros((1, tq), jnp.float32),
                      jnp.zeros((dv, tq), jnp.float32)) for _ in hs)
        res = lax.fori_loop(0, nk, body, init)
        for (m, l, acc), h in zip(res, hs):
            ot_ref[0, h * dv:(h + 1) * dv, :] = (acc / l).astype(ot_ref.dtype)


def _attention(qt, k, vt, *, groups_per_step, rep, dv, name):
    B, _, S = qt.shape
    G = k.shape[1]
    gs = groups_per_step
    heads = gs * rep
    tq = TQ_ATTN
    kern = functools.partial(_attn_kernel, heads=heads, rep=rep, dv=dv, tk=TK_ATTN,
                             heads_per_loop=min(2, heads))
    return pl.pallas_call(
        kern,
        grid=(B, G // gs, S // tq),
        in_specs=[
            pl.BlockSpec((1, heads * DQ_PAD, tq), lambda b, g, i: (b, g, i)),
            pl.BlockSpec((1, gs, S, DQ_PAD), lambda b, g, i: (b, g, 0, 0)),
            pl.BlockSpec((1, gs * dv, S), lambda b, g, i: (b, g, 0)),
        ],
        out_specs=pl.BlockSpec((1, heads * dv, tq), lambda b, g, i: (b, g, i)),
        out_shape=jax.ShapeDtypeStruct((B, G * rep * dv, S), MM_DTYPE),
        compiler_params=_cparams(("parallel", "parallel", "arbitrary")),
        name=name,
    )(qt, k, vt)


def _post_kernel(x_ref, ln1_ref, wgt_ref, oat_ref, obt_ref, wbrat_ref, wbrbt_ref, woutt_ref, g2_ref,
                 ht_ref, xn2t_ref):
    bf = MM_DTYPE
    x = x_ref[0]
    xn = x * lax.rsqrt(jnp.mean(x * x, axis=-1, keepdims=True) + EPS) * ln1_ref[...]
    xnt = xn.T.astype(bf)
    gates = jax.nn.sigmoid(_dot(wgt_ref[...], xnt))
    ma = _dot(wbrat_ref[...], oat_ref[0])
    mb = _dot(wbrbt_ref[...], obt_ref[0])
    merged = gates[:D_MODEL] * ma + gates[D_MODEL:] * mb
    ht = x.T + _dot(woutt_ref[...], merged.astype(bf))
    ht_ref[0] = ht
    r2 = lax.rsqrt(jnp.mean(ht * ht, axis=0, keepdims=True) + EPS)
    xn2t_ref[0] = (ht * r2 * g2_ref[...]).astype(bf)


def _post(x, oat, obt, W):
    B, S, _ = x.shape
    tm = TM_POST
    const = lambda shape: pl.BlockSpec(shape, lambda b, i: (0,) * len(shape))
    return pl.pallas_call(
        _post_kernel,
        grid=(B, S // tm),
        in_specs=[
            pl.BlockSpec((1, tm, D_MODEL), lambda b, i: (b, i, 0)),
            const((1, D_MODEL)), const((2 * D_MODEL, D_MODEL)),
            pl.BlockSpec((1, H_A * HD_A, tm), lambda b, i: (b, 0, i)),
            pl.BlockSpec((1, H_B * V_B, tm), lambda b, i: (b, 0, i)),
            const((D_MODEL, H_A * HD_A)), const((D_MODEL, H_B * V_B)), const((D_MODEL, D_MODEL)),
            const((D_MODEL, tm)),
        ],
        out_specs=[
            pl.BlockSpec((1, D_MODEL, tm), lambda b, i: (b, 0, i)),
            pl.BlockSpec((1, D_MODEL, tm), lambda b, i: (b, 0, i)),
        ],
        out_shape=[
            jax.ShapeDtypeStruct((B, D_MODEL, S), jnp.float32),
            jax.ShapeDtypeStruct((B, D_MODEL, S), MM_DTYPE),
        ],
        compiler_params=_cparams(("parallel", "parallel")),
        name="post",
    )(x, W["ln1"], W["wgt"], oat, obt, W["wbrat"], W["wbrbt"], W["woutt"], W["g2"])


def _cmpx(xs, i, j):
    a, b = xs[i], xs[j]
    xs[i], xs[j] = jnp.maximum(a, b), jnp.minimum(a, b)


def _bitonic_merge_desc(xs):
    n = len(xs)
    j = n // 2
    while j >= 1:
        for i in range(n):
            l = i ^ j
            if l > i:
                _cmpx(xs, i, l)
        j //= 2


def _bitonic_sort_desc(xs):
    n = len(xs)
    k = 2
    while k <= n:
        j = k // 2
        while j >= 1:
            for i in range(n):
                l = i ^ j
                if l > i:
                    if (i & k) == 0:
                        _cmpx(xs, i, l)
                    else:
                        _cmpx(xs, l, i)
            j //= 2
        k *= 2


_CAND_PAIRS = [(i, j) for i in range(PEER_TOPK) for j in range(PEER_TOPK) if (i + 1) * (j + 1) <= PEER_TOPK]


def _route_kernel(xn2t_ref, wpqt_ref, subk_ref, s_ref, stats_ref):
    bf = MM_DTYPE
    tr = xn2t_ref.shape[2]
    groups = tr // LANES
    qt = _dot(wpqt_ref[...], xn2t_ref[0]).astype(bf)
    row = lax.broadcasted_iota(jnp.int32, (SUBLANES, LANES), 0)
    nblk = N_KEYS // SUBLANES
    for h in range(PEER_HEADS):
        tops = []
        for pp in range(2):
            hp = h * 2 + pp
            st = _dot(subk_ref[hp], qt[hp * HALF_KEY:(hp + 1) * HALF_KEY])
            s_ref[0, hp * N_KEYS:(hp + 1) * N_KEYS, :] = st
            xs = [st[v * SUBLANES:(v + 1) * SUBLANES] for v in range(nblk)]
            _bitonic_sort_desc(xs)
            for shift in (4, 2, 1):
                rolled = [pltpu.roll(xs[nblk - 1 - v], shift, 0) for v in range(nblk)]
                xs = [jnp.maximum(a, b) for a, b in zip(xs, rolled)]
                _bitonic_merge_desc(xs)
            comp = []
            for v in range(PEER_TOPK):
                c = xs[v][:, 0:LANES]
                for gidx in range(1, groups):
                    c = jnp.where(row == gidx, xs[v][:, gidx * LANES:(gidx + 1) * LANES], c)
                comp.append(c)
            tops.append(comp)
        a, b = tops
        cands = [a[i] + b[j] for (i, j) in _CAND_PAIRS]
        cands += [jnp.full((SUBLANES, LANES), -jnp.inf, jnp.float32)] * (64 - len(cands))
        _bitonic_sort_desc(cands)
        cmax = cands[0]
        z = jnp.zeros((SUBLANES, LANES), jnp.float32)
        for kk in range(PEER_TOPK):
            z = z + jnp.exp(cands[kk] - cmax)
        for r, val in enumerate((cands[PEER_TOPK - 1], a[0], b[0], z)):
            wide = jnp.concatenate([val[gidx:gidx + 1, :] for gidx in range(groups)], axis=1)
            stats_ref[0, h * 4 + r:h * 4 + r + 1, :] = wide


def _route(xn2t, W):
    B, _, S = xn2t.shape
    tr = TR_ROUTE
    const = lambda shape: pl.BlockSpec(shape, lambda b, i: (0,) * len(shape))
    return pl.pallas_call(
        _route_kernel,
        grid=(B, S // tr),
        in_specs=[
            pl.BlockSpec((1, D_MODEL, tr), lambda b, i: (b, 0, i)),
            const((PEER_HEADS * D_KEY, D_MODEL)),
            const((PEER_HEADS * 2, N_KEYS, HALF_KEY)),
        ],
        out_specs=[
            pl.BlockSpec((1, PEER_HEADS * 2 * N_KEYS, tr), lambda b, i: (b, 0, i)),
            pl.BlockSpec((1, PEER_HEADS * 4, tr), lambda b, i: (b, 0, i)),
        ],
        out_shape=[
            jax.ShapeDtypeStruct((B, PEER_HEADS * 2 * N_KEYS, S), jnp.float32),
            jax.ShapeDtypeStruct((B, PEER_HEADS * 4, S), jnp.float32),
        ],
        compiler_params=_cparams(("parallel", "parallel")),
        name="route",
    )(xn2t, W["wpqt"], W["subk"])


def _peer_kernel(xn2t_ref, s_ref, stats_ref, ht_ref, u_ref, vt_ref, gfin_ref, y_ref,
                 e1_ref, e2_ref, acc_ref):
    e = pl.program_id(2)
    ne = pl.num_programs(2)
    ec = u_ref.shape[0]
    tt = xn2t_ref.shape[2]
    rows_per_chunk = ec // N_KEYS

    @pl.when(e == 0)
    def _():
        for h in range(PEER_HEADS):
            amax = stats_ref[0, h * 4 + 1:h * 4 + 2, :]
            bmax = stats_ref[0, h * 4 + 2:h * 4 + 3, :]
            z = stats_ref[0, h * 4 + 3:h * 4 + 4, :]
            s1 = s_ref[0, (2 * h) * N_KEYS:(2 * h + 1) * N_KEYS, :]
            s2 = s_ref[0, (2 * h + 1) * N_KEYS:(2 * h + 2) * N_KEYS, :]
            e1_ref[h * N_KEYS:(h + 1) * N_KEYS, :] = jnp.exp(s1 - amax)
            e2_ref[h * N_KEYS:(h + 1) * N_KEYS, :] = jnp.exp(s2 - bmax) / z
        acc_ref[...] = jnp.zeros_like(acc_ref)

    hid = _dot(u_ref[...], xn2t_ref[0])
    act = 0.5 * hid * (1.0 + lax.erf(hid * (2.0 ** -0.5)))
    parts = []
    for r in range(rows_per_chunk):
        i = e * rows_per_chunk + r
        w = jnp.zeros((N_KEYS, tt), jnp.float32)
        for h in range(PEER_HEADS):
            tau = stats_ref[0, h * 4:h * 4 + 1, :]
            s1row = s_ref[0, pl.ds(2 * h * N_KEYS + i, 1), :]
            e1row = e1_ref[pl.ds(h * N_KEYS + i, 1), :]
            s2 = s_ref[0, (2 * h + 1) * N_KEYS:(2 * h + 2) * N_KEYS, :]
            e2 = e2_ref[h * N_KEYS:(h + 1) * N_KEYS, :]
            w = w + jnp.where(s1row + s2 >= tau, e1row * e2, 0.0)
        parts.append(w)
    wfull = jnp.concatenate(parts, axis=0) if rows_per_chunk > 1 else parts[0]
    acc_ref[...] += _dot(vt_ref[...], (act * wfull).astype(MM_DTYPE))

    @pl.when(e == ne - 1)
    def _():
        yt = ht_ref[0] + acc_ref[...]
        r = lax.rsqrt(jnp.mean(yt * yt, axis=0, keepdims=True) + EPS)
        y_ref[0] = (yt * r * gfin_ref[...]).T


def _peer(xn2t, s, stats, ht, W):
    B, _, S = xn2t.shape
    tt, ec = TT_PEER, EC_PEER
    return pl.pallas_call(
        _peer_kernel,
        grid=(B, S // tt, N_EXPERTS // ec),
        in_specs=[
            pl.BlockSpec((1, D_MODEL, tt), lambda b, i, e: (b, 0, i)),
            pl.BlockSpec((1, PEER_HEADS * 2 * N_KEYS, tt), lambda b, i, e: (b, 0, i)),
            pl.BlockSpec((1, PEER_HEADS * 4, tt), lambda b, i, e: (b, 0, i)),
            pl.BlockSpec((1, D_MODEL, tt), lambda b, i, e: (b, 0, i)),
            pl.BlockSpec((ec, D_MODEL), lambda b, i, e: (e, 0)),
            pl.BlockSpec((D_MODEL, ec), lambda b, i, e: (0, e)),
            pl.BlockSpec((D_MODEL, tt), lambda b, i, e: (0, 0)),
        ],
        out_specs=pl.BlockSpec((1, tt, D_MODEL), lambda b, i, e: (b, i, 0)),
        out_shape=jax.ShapeDtypeStruct((B, S, D_MODEL), jnp.float32),
        scratch_shapes=[
            pltpu.VMEM((PEER_HEADS * N_KEYS, tt), jnp.float32),
            pltpu.VMEM((PEER_HEADS * N_KEYS, tt), jnp.float32),
            pltpu.VMEM((D_MODEL, tt), jnp.float32),
        ],
        compiler_params=_cparams(("parallel", "parallel", "arbitrary")),
        name="peer",
    )(xn2t, s, stats, ht, W["u"], W["vt"], W["gfin"])


def _forward(x, W):
    qat, ka, vat, qbt, kb, vbt = _proj(x, W)
    oat = _attention(qat, ka, vat, groups_per_step=1, rep=H_A // G_A, dv=HD_A, name="attn_gqa")
    obt = _attention(qbt, kb, vbt, groups_per_step=2, rep=1, dv=V_B, name="attn_mla")
    ht, xn2t = _post(x, oat, obt, W)
    s, stats = _route(xn2t, W)
    return _peer(xn2t, s, stats, ht, W)


def kernel(x_prompt, x_sample, ln1_g, w_in, qa_norm_g, ka_norm_g, cq_norm_g, ckv_norm_g, w_uq, w_ukv,
           w_br_a, w_br_b, w_out, ln2_g, w_pq, sub_keys, expert_u, expert_v, final_g):
    params = (ln1_g, w_in, qa_norm_g, ka_norm_g, cq_norm_g, ckv_norm_g, w_uq, w_ukv,
              w_br_a, w_br_b, w_out, ln2_g, w_pq, sub_keys, expert_u, expert_v, final_g)
    assert x_prompt.shape[1] == x_sample.shape[1]
    W = _prepare(x_prompt.shape[1], *params)
    return (_forward(x_prompt, W), _forward(x_sample, W))
```

```python
import functools
import math

import numpy as np
import jax
import jax.numpy as jnp
from jax import lax
from jax.experimental import pallas as pl
from jax.experimental.pallas import tpu as pltpu

D_MODEL = 1024
GRID_W = 64
ROPE_THETA = 10000.0
EPS = 1e-6
H_A, G_A, HD_A = 8, 2, 64
H_B, Q_LORA, KV_LORA, NOPE_B, ROPE_B, V_B = 8, 256, 128, 64, 32, 64
PEER_HEADS, N_KEYS, D_KEY, PEER_TOPK = 8, 128, 256, 16
N_EXPERTS = N_KEYS * N_KEYS
HALF_KEY = D_KEY // 2

LANES = 128
SUBLANES = 8
VMEM_LIMIT_BYTES = 56 * 1024 * 1024

DQ_PAD = LANES
LOG2E = math.log2(math.e)
MM_DTYPE = jnp.bfloat16
ONES_ROWS = 2 * SUBLANES
VROWS_A = HD_A + ONES_ROWS
VROWS_B = V_B + ONES_ROWS
SAFE_SCORE_BOUND = 60.0

TM_PROJ = 512
TQ_ATTN = 512
TK_ATTN = 256
TM_POST = 512
TR_ROUTE = SUBLANES * LANES
TT_PEER = 512
EC_PEER = 1024
PEER_TJ = 32

_R_QA, _R_QA2, _R_KA, _R_KA2, _R_VA, _R_CQ, _R_CKV, _R_KR, _R_KR2, _R_END = (
    0, 512, 1024, 1152, 1280, 1408, 1664, 1792, 1824, 1856)
_T_QAC, _T_QAS, _T_KAC, _T_KAS, _T_QBC, _T_QBS, _T_KBC, _T_KBS, _T_END = (
    0, 64, 128, 192, 256, 288, 320, 352, 384)


def _cparams(sem):
    return pltpu.CompilerParams(dimension_semantics=sem, vmem_limit_bytes=VMEM_LIMIT_BYTES)


def _dot(a, b):
    return jnp.dot(a, b, preferred_element_type=jnp.float32)


def _rope_partner(d):
    half, qtr = d // 2, d // 4
    e = np.arange(d)
    within = e % half
    first = within < qtr
    partner = np.where(first, e + qtr, e - qtr)
    sign = np.where(first, -1.0, 1.0).astype(np.float32)
    freq_idx = within % qtr
    use_col = e >= half
    return partner, sign, freq_idx, use_col


def _rope_tables(S, d):
    half = d // 2
    partner, sign, freq_idx, use_col = _rope_partner(d)
    t = jnp.arange(S, dtype=jnp.int32)
    r = (t // GRID_W).astype(jnp.float32)
    c = (t % GRID_W).astype(jnp.float32)
    freqs = ROPE_THETA ** (-jnp.arange(0, half, 2, dtype=jnp.float32) / half)
    f = freqs[freq_idx]
    pos = jnp.where(jnp.asarray(use_col)[:, None], c[None, :], r[None, :])
    ang = pos * f[:, None]
    return jnp.cos(ang), jnp.sin(ang) * jnp.asarray(sign)[:, None], partner


def _prepare(S, ln1_g, w_in, qa_norm_g, ka_norm_g, cq_norm_g, ckv_norm_g, w_uq, w_ukv,
             w_br_a, w_br_b, w_out, ln2_g, w_pq, sub_keys, expert_u, expert_v, final_g):
    bf = MM_DTYPE
    w = w_in[0]
    c_qa, c_ka, c_va, c_cq, c_ckv, c_kr, c_ga = np.cumsum([512, 128, 128, 256, 128, 32, 1024]).tolist()
    p64, _, _, _ = _rope_partner(HD_A)
    p32, _, _, _ = _rope_partner(ROPE_B)
    qa_perm = (np.arange(H_A)[:, None] * HD_A + p64[None, :]).reshape(-1)
    ka_perm = (np.arange(G_A)[:, None] * HD_A + p64[None, :]).reshape(-1)
    w_qa, w_ka = w[:, :c_qa], w[:, c_qa:c_ka]
    w_kr = w[:, c_ckv:c_kr]
    w1 = jnp.concatenate([
        w_qa, w_qa[:, qa_perm], w_ka, w_ka[:, ka_perm], w[:, c_ka:c_va],
        w[:, c_va:c_cq], w[:, c_cq:c_ckv], w_kr, w_kr[:, p32]], axis=1)
    w1t = w1.T.astype(bf)
    wgt = w[:, c_kr:].T.astype(bf)

    uq = w_uq[0].reshape(Q_LORA, H_B, NOPE_B + ROPE_B)
    uq_nope = uq[:, :, :NOPE_B].reshape(Q_LORA, H_B * NOPE_B)
    uq_rope = uq[:, :, NOPE_B:]
    wuqt = jnp.concatenate([uq_nope, uq_rope.reshape(Q_LORA, -1),
                            uq_rope[:, :, p32].reshape(Q_LORA, -1)], axis=1).T.astype(bf)
    ukv = w_ukv[0].reshape(KV_LORA, H_B, NOPE_B + V_B)
    wukvt = jnp.concatenate([ukv[:, :, :NOPE_B].reshape(KV_LORA, -1),
                             ukv[:, :, NOPE_B:].reshape(KV_LORA, -1)], axis=1).T.astype(bf)

    cos64, sin64, _ = _rope_tables(S, HD_A)
    cos32, sin32, _ = _rope_tables(S, ROPE_B)
    sa = (HD_A ** -0.5) * LOG2E
    sb = ((NOPE_B + ROPE_B) ** -0.5) * LOG2E
    gq, gk = qa_norm_g[0], ka_norm_g[0]
    tabs = jnp.concatenate([
        cos64 * (gq * sa)[:, None], sin64 * (gq[p64] * sa)[:, None],
        cos64 * gk[:, None], sin64 * gk[p64][:, None],
        cos32 * sb, sin32 * sb, cos32, sin32], axis=0)

    def bcast(g, n):
        return jnp.broadcast_to(g.reshape(-1, 1), (g.size, n))

    return dict(
        ln1=ln1_g[0].reshape(1, D_MODEL), w1t=w1t, wgt=wgt, wuqt=wuqt, wukvt=wukvt, tabs=tabs,
        gcq=bcast(cq_norm_g[0], TM_PROJ), gckv=bcast(ckv_norm_g[0], TM_PROJ),
        wbrat=w_br_a[0].T.astype(bf), wbrbt=w_br_b[0].T.astype(bf), woutt=w_out[0].T.astype(bf),
        g2=bcast(ln2_g[0], TM_POST), wpqt=w_pq[0].T.astype(bf),
        subk=sub_keys[0].reshape(PEER_HEADS * 2, N_KEYS, HALF_KEY).astype(bf),
        u=expert_u[0].astype(bf), vt=expert_v[0].T.astype(bf),
        gfin=bcast(final_g, TT_PEER), sb=sb)


def _proj_kernel(x_ref, ln1_ref, w1t_ref, tab_ref, gcq_ref, gckv_ref, wuqt_ref, wukvt_ref,
                 qat_ref, ka_ref, vat_ref, qan_ref, kan_ref, qbt_ref, kb_ref, vbt_ref, qbn_ref, kbn_ref, *, sb):
    bf = MM_DTYPE
    x = x_ref[0]
    tm = x.shape[0]
    xn = x * lax.rsqrt(jnp.mean(x * x, axis=-1, keepdims=True) + EPS) * ln1_ref[...]
    xnt = xn.T.astype(bf)
    p = _dot(w1t_ref[...], xnt)
    tab = tab_ref[...]
    zeros64 = jnp.zeros((64, tm), jnp.float32)
    zeros32 = jnp.zeros((32, tm), jnp.float32)
    ones = jnp.ones((ONES_ROWS, tm), bf)

    def rstd(y):
        return lax.rsqrt(jnp.mean(y * y, axis=0, keepdims=True) + EPS)

    def sqnorm(yb):
        yf = yb.astype(jnp.float32)
        return jnp.sum(yf * yf, axis=0, keepdims=True)

    rep = H_A // G_A
    for h in range(H_A):
        y = p[_R_QA + h * HD_A:_R_QA + (h + 1) * HD_A]
        y2 = p[_R_QA2 + h * HD_A:_R_QA2 + (h + 1) * HD_A]
        q = ((y * tab[_T_QAC:_T_QAS] + y2 * tab[_T_QAS:_T_KAC]) * rstd(y)).astype(bf)
        qat_ref[0, h * DQ_PAD:h * DQ_PAD + HD_A, :] = q
        qat_ref[0, h * DQ_PAD + HD_A:(h + 1) * DQ_PAD, :] = zeros64.astype(bf)
        qan_ref[0, h // rep, h % rep:h % rep + 1, :] = sqnorm(q)
    for g in range(G_A):
        y = p[_R_KA + g * HD_A:_R_KA + (g + 1) * HD_A]
        y2 = p[_R_KA2 + g * HD_A:_R_KA2 + (g + 1) * HD_A]
        k = (y * tab[_T_KAC:_T_KAS] + y2 * tab[_T_KAS:_T_QBC]) * rstd(y)
        ka_ref[0, g] = jnp.concatenate([k, zeros64], axis=0).T.astype(bf)
        kan_ref[0, g, 0:1, :] = sqnorm(k.astype(bf))
        vat_ref[0, g * VROWS_A:g * VROWS_A + HD_A, :] = p[_R_VA + g * HD_A:_R_VA + (g + 1) * HD_A].astype(bf)
        vat_ref[0, g * VROWS_A + HD_A:(g + 1) * VROWS_A, :] = ones

    cq = p[_R_CQ:_R_CKV]
    cqn = (cq * rstd(cq) * gcq_ref[...]).astype(bf)
    q2 = _dot(wuqt_ref[...], cqn)
    for h in range(H_B):
        nope = (q2[h * NOPE_B:(h + 1) * NOPE_B] * sb).astype(bf)
        yr = q2[512 + h * ROPE_B:512 + (h + 1) * ROPE_B]
        yr2 = q2[768 + h * ROPE_B:768 + (h + 1) * ROPE_B]
        rope = (yr * tab[_T_QBC:_T_QBS] + yr2 * tab[_T_QBS:_T_KBC]).astype(bf)
        base = h * DQ_PAD
        qbt_ref[0, base:base + NOPE_B, :] = nope
        qbt_ref[0, base + NOPE_B:base + NOPE_B + ROPE_B, :] = rope
        qbt_ref[0, base + NOPE_B + ROPE_B:base + DQ_PAD, :] = zeros32.astype(bf)
        qbn_ref[0, h // 2, h % 2:h % 2 + 1, :] = sqnorm(nope) + sqnorm(rope)
    ckv = p[_R_CKV:_R_KR]
    ckvn = (ckv * rstd(ckv) * gckv_ref[...]).astype(bf)
    kv = _dot(wukvt_ref[...], ckvn)
    krope = p[_R_KR:_R_KR2] * tab[_T_KBC:_T_KBS] + p[_R_KR2:_R_END] * tab[_T_KBS:_T_END]
    krope_n = sqnorm(krope.astype(bf))
    for h in range(H_B):
        knope = kv[h * NOPE_B:(h + 1) * NOPE_B]
        kt = jnp.concatenate([knope, krope, zeros32], axis=0)
        kb_ref[0, h] = kt.T.astype(bf)
        kbn_ref[0, h // 2, h % 2:h % 2 + 1, :] = sqnorm(knope.astype(bf)) + krope_n
        v0 = H_B * NOPE_B + h * V_B
        vbt_ref[0, h * VROWS_B:h * VROWS_B + V_B, :] = kv[v0:v0 + V_B].astype(bf)
        vbt_ref[0, h * VROWS_B + V_B:(h + 1) * VROWS_B, :] = ones


def _proj(x, W):
    B, S, _ = x.shape
    tm = TM_PROJ
    bf = MM_DTYPE
    f32 = jnp.float32
    rep = H_A // G_A
    const = lambda shape: pl.BlockSpec(shape, lambda b, i: (0,) * len(shape))
    return pl.pallas_call(
        functools.partial(_proj_kernel, sb=W["sb"]),
        grid=(B, S // tm),
        in_specs=[
            pl.BlockSpec((1, tm, D_MODEL), lambda b, i: (b, i, 0)),
            const((1, D_MODEL)), const((_R_END, D_MODEL)),
            pl.BlockSpec((_T_END, tm), lambda b, i: (0, i)),
            const((Q_LORA, tm)), const((KV_LORA, tm)),
            const((1024, Q_LORA)), const((1024, KV_LORA)),
        ],
        out_specs=[
            pl.BlockSpec((1, H_A * DQ_PAD, tm), lambda b, i: (b, 0, i)),
            pl.BlockSpec((1, G_A, tm, DQ_PAD), lambda b, i: (b, 0, i, 0)),
            pl.BlockSpec((1, G_A * VROWS_A, tm), lambda b, i: (b, 0, i)),
            pl.BlockSpec((1, G_A, rep, tm), lambda b, i: (b, 0, 0, i)),
            pl.BlockSpec((1, G_A, 1, tm), lambda b, i: (b, 0, 0, i)),
            pl.BlockSpec((1, H_B * DQ_PAD, tm), lambda b, i: (b, 0, i)),
            pl.BlockSpec((1, H_B, tm, DQ_PAD), lambda b, i: (b, 0, i, 0)),
            pl.BlockSpec((1, H_B * VROWS_B, tm), lambda b, i: (b, 0, i)),
            pl.BlockSpec((1, H_B // 2, 2, tm), lambda b, i: (b, 0, 0, i)),
            pl.BlockSpec((1, H_B // 2, 2, tm), lambda b, i: (b, 0, 0, i)),
        ],
        out_shape=[
            jax.ShapeDtypeStruct((B, H_A * DQ_PAD, S), bf),
            jax.ShapeDtypeStruct((B, G_A, S, DQ_PAD), bf),
            jax.ShapeDtypeStruct((B, G_A * VROWS_A, S), bf),
            jax.ShapeDtypeStruct((B, G_A, rep, S), f32),
            jax.ShapeDtypeStruct((B, G_A, 1, S), f32),
            jax.ShapeDtypeStruct((B, H_B * DQ_PAD, S), bf),
            jax.ShapeDtypeStruct((B, H_B, S, DQ_PAD), bf),
            jax.ShapeDtypeStruct((B, H_B * VROWS_B, S), bf),
            jax.ShapeDtypeStruct((B, H_B // 2, 2, S), f32),
            jax.ShapeDtypeStruct((B, H_B // 2, 2, S), f32),
        ],
        compiler_params=_cparams(("parallel", "parallel")),
        name="proj",
    )(x, W["ln1"], W["w1t"], W["tabs"], W["gcq"], W["gckv"], W["wuqt"], W["wukvt"])


def _attn_kernel(qt_ref, k_ref, vt_ref, qn_ref, kn_ref, ot_ref, s_ref, *, heads, rep, dv, tk, heads_per_loop):
    S = k_ref.shape[2]
    tq = qt_ref.shape[2]
    nk = S // tk
    vrows = dv + ONES_ROWS

    def scores(h, c):
        off = pl.multiple_of(c * tk, tk)
        kc = k_ref[0, h // rep, pl.ds(off, tk), :]
        return _dot(kc, qt_ref[0, h * DQ_PAD:(h + 1) * DQ_PAD, :])

    def vchunk(h, c):
        off = pl.multiple_of(c * tk, tk)
        g = h // rep
        return vt_ref[0, g * vrows:(g + 1) * vrows, pl.ds(off, tk)]

    def finish(h, acc):
        ot_ref[0, h * dv:(h + 1) * dv, :] = (acc[:dv] / acc[dv:dv + 1]).astype(ot_ref.dtype)

    zero_acc = lambda: jnp.zeros((vrows, tq), jnp.float32)
    bound2 = jnp.max(qn_ref[0, 0]) * jnp.max(kn_ref[0, 0])
    safe = bound2 <= SAFE_SCORE_BOUND * SAFE_SCORE_BOUND

    def softmax_pv(h, c, s, state, with_max):
        m, acc = state
        if not with_max:
            return (m, acc + _dot(vchunk(h, c), jnp.exp2(s).astype(MM_DTYPE)))
        mn = jnp.maximum(m, jnp.max(s, axis=0, keepdims=True))
        alpha = jnp.exp2(m - mn)
        pexp = jnp.exp2(s - mn)
        return (mn, alpha * acc + _dot(vchunk(h, c), pexp.astype(MM_DTYPE)))

    def run(with_max):
        for h0 in range(0, heads, heads_per_loop):
            hs = list(range(h0, h0 + heads_per_loop))
            for i, h in enumerate(hs):
                s_ref[i] = scores(h, 0)

            def body(j, carry, hs=hs):
                c0 = 2 * j
                ahead = [scores(h, c0 + 1) for h in hs]
                carry = [softmax_pv(h, c0, s_ref[i], st, with_max) for i, (h, st) in enumerate(zip(hs, carry))]
                nxt = jnp.minimum(c0 + 2, nk - 1)
                for i, h in enumerate(hs):
                    s_ref[i] = scores(h, nxt)
                carry = [softmax_pv(h, c0 + 1, ahead[i], st, with_max) for i, (h, st) in enumerate(zip(hs, carry))]
                return tuple(carry)

            init = tuple((jnp.full((1, tq), -jnp.inf, jnp.float32), zero_acc()) for _ in hs)
            res = lax.fori_loop(0, nk // 2, body, init)
            for (m, acc), h in zip(res, hs):
                finish(h, acc)

    pl.when(safe)(lambda: run(False))
    pl.when(jnp.logical_not(safe))(lambda: run(True))


def _attention(qt, k, vt, qn, kn, *, groups_per_step, rep, dv, name):
    B, _, S = qt.shape
    G = k.shape[1]
    gs = groups_per_step
    heads = gs * rep
    tq = TQ_ATTN
    hpl = min(2, heads)
    vrows = dv + ONES_ROWS
    assert (S // TK_ATTN) % 2 == 0
    kern = functools.partial(_attn_kernel, heads=heads, rep=rep, dv=dv, tk=TK_ATTN, heads_per_loop=hpl)
    return pl.pallas_call(
        kern,
        grid=(B, G // gs, S // tq),
        in_specs=[
            pl.BlockSpec((1, heads * DQ_PAD, tq), lambda b, g, i: (b, g, i)),
            pl.BlockSpec((1, gs, S, DQ_PAD), lambda b, g, i: (b, g, 0, 0)),
            pl.BlockSpec((1, gs * vrows, S), lambda b, g, i: (b, g, 0)),
            pl.BlockSpec((1, 1, heads, tq), lambda b, g, i: (b, g, 0, i)),
            pl.BlockSpec((1, 1, gs, S), lambda b, g, i: (b, g, 0, 0)),
        ],
        out_specs=pl.BlockSpec((1, heads * dv, tq), lambda b, g, i: (b, g, i)),
        out_shape=jax.ShapeDtypeStruct((B, G * rep * dv, S), MM_DTYPE),
        scratch_shapes=[pltpu.VMEM((hpl, TK_ATTN, tq), jnp.float32)],
        compiler_params=_cparams(("parallel", "parallel", "arbitrary")),
        name=name,
    )(qt, k, vt, qn, kn)


def _post_kernel(x_ref, ln1_ref, wgt_ref, oat_ref, obt_ref, wbrat_ref, wbrbt_ref, woutt_ref, g2_ref,
                 ht_ref, xn2t_ref):
    bf = MM_DTYPE
    x = x_ref[0]
    xn = x * lax.rsqrt(jnp.mean(x * x, axis=-1, keepdims=True) + EPS) * ln1_ref[...]
    xnt = xn.T.astype(bf)
    gates = jax.nn.sigmoid(_dot(wgt_ref[...], xnt))
    ma = _dot(wbrat_ref[...], oat_ref[0])
    mb = _dot(wbrbt_ref[...], obt_ref[0])
    merged = gates[:D_MODEL] * ma + gates[D_MODEL:] * mb
    ht = x.T + _dot(woutt_ref[...], merged.astype(bf))
    ht_ref[0] = ht
    r2 = lax.rsqrt(jnp.mean(ht * ht, axis=0, keepdims=True) + EPS)
    xn2t_ref[0] = (ht * r2 * g2_ref[...]).astype(bf)


def _post(x, oat, obt, W):
    B, S, _ = x.shape
    tm = TM_POST
    const = lambda shape: pl.BlockSpec(shape, lambda b, i: (0,) * len(shape))
    return pl.pallas_call(
        _post_kernel,
        grid=(B, S // tm),
        in_specs=[
            pl.BlockSpec((1, tm, D_MODEL), lambda b, i: (b, i, 0)),
            const((1, D_MODEL)), const((2 * D_MODEL, D_MODEL)),
            pl.BlockSpec((1, H_A * HD_A, tm), lambda b, i: (b, 0, i)),
            pl.BlockSpec((1, H_B * V_B, tm), lambda b, i: (b, 0, i)),
            const((D_MODEL, H_A * HD_A)), const((D_MODEL, H_B * V_B)), const((D_MODEL, D_MODEL)),
            const((D_MODEL, tm)),
        ],
        out_specs=[
            pl.BlockSpec((1, D_MODEL, tm), lambda b, i: (b, 0, i)),
            pl.BlockSpec((1, D_MODEL, tm), lambda b, i: (b, 0, i)),
        ],
        out_shape=[
            jax.ShapeDtypeStruct((B, D_MODEL, S), jnp.float32),
            jax.ShapeDtypeStruct((B, D_MODEL, S), MM_DTYPE),
        ],
        compiler_params=_cparams(("parallel", "parallel")),
        name="post",
    )(x, W["ln1"], W["wgt"], oat, obt, W["wbrat"], W["wbrbt"], W["woutt"], W["g2"])


def _cmpx(xs, i, j):
    a, b = xs[i], xs[j]
    xs[i], xs[j] = jnp.maximum(a, b), jnp.minimum(a, b)


def _bitonic_merge_desc(xs):
    n = len(xs)
    j = n // 2
    while j >= 1:
        for i in range(n):
            l = i ^ j
            if l > i:
                _cmpx(xs, i, l)
        j //= 2


def _bitonic_sort_desc(xs):
    n = len(xs)
    k = 2
    while k <= n:
        j = k // 2
        while j >= 1:
            for i in range(n):
                l = i ^ j
                if l > i:
                    if (i & k) == 0:
                        _cmpx(xs, i, l)
                    else:
                        _cmpx(xs, l, i)
            j //= 2
        k *= 2


_CAND_PAIRS = [(i, j) for i in range(PEER_TOPK) for j in range(PEER_TOPK) if (i + 1) * (j + 1) <= PEER_TOPK]


def _route_kernel(xn2t_ref, wpqt_ref, subk_ref, s_ref, stats_ref):
    bf = MM_DTYPE
    tr = xn2t_ref.shape[2]
    groups = tr // LANES
    qt = _dot(wpqt_ref[...], xn2t_ref[0]).astype(bf)
    row = lax.broadcasted_iota(jnp.int32, (SUBLANES, LANES), 0)
    nblk = N_KEYS // SUBLANES
    for h in range(PEER_HEADS):
        tops = []
        for pp in range(2):
            hp = h * 2 + pp
            st = _dot(subk_ref[hp], qt[hp * HALF_KEY:(hp + 1) * HALF_KEY])
            s_ref[0, hp * N_KEYS:(hp + 1) * N_KEYS, :] = st
            xs = [st[v * SUBLANES:(v + 1) * SUBLANES] for v in range(nblk)]
            _bitonic_sort_desc(xs)
            for shift in (4, 2, 1):
                rolled = [pltpu.roll(xs[nblk - 1 - v], shift, 0) for v in range(nblk)]
                xs = [jnp.maximum(a, b) for a, b in zip(xs, rolled)]
                _bitonic_merge_desc(xs)
            comp = []
            for v in range(PEER_TOPK):
                c = xs[v][:, 0:LANES]
                for gidx in range(1, groups):
                    c = jnp.where(row == gidx, xs[v][:, gidx * LANES:(gidx + 1) * LANES], c)
                comp.append(c)
            tops.append(comp)
        a, b = tops
        cands = [a[i] + b[j] for (i, j) in _CAND_PAIRS]
        cands += [jnp.full((SUBLANES, LANES), -jnp.inf, jnp.float32)] * (64 - len(cands))
        _bitonic_sort_desc(cands)
        cmax = cands[0]
        z = jnp.zeros((SUBLANES, LANES), jnp.float32)
        for kk in range(PEER_TOPK):
            z = z + jnp.exp(cands[kk] - cmax)
        for r, val in enumerate((cands[PEER_TOPK - 1], a[0], b[0], z)):
            wide = jnp.concatenate([val[gidx:gidx + 1, :] for gidx in range(groups)], axis=1)
            stats_ref[0, h * 4 + r:h * 4 + r + 1, :] = wide


def _route(xn2t, W):
    B, _, S = xn2t.shape
    tr = TR_ROUTE
    const = lambda shape: pl.BlockSpec(shape, lambda b, i: (0,) * len(shape))
    return pl.pallas_call(
        _route_kernel,
        grid=(B, S // tr),
        in_specs=[
            pl.BlockSpec((1, D_MODEL, tr), lambda b, i: (b, 0, i)),
            const((PEER_HEADS * D_KEY, D_MODEL)),
            const((PEER_HEADS * 2, N_KEYS, HALF_KEY)),
        ],
        out_specs=[
            pl.BlockSpec((1, PEER_HEADS * 2 * N_KEYS, tr), lambda b, i: (b, 0, i)),
            pl.BlockSpec((1, PEER_HEADS * 4, tr), lambda b, i: (b, 0, i)),
        ],
        out_shape=[
            jax.ShapeDtypeStruct((B, PEER_HEADS * 2 * N_KEYS, S), jnp.float32),
            jax.ShapeDtypeStruct((B, PEER_HEADS * 4, S), jnp.float32),
        ],
        compiler_params=_cparams(("parallel", "parallel")),
        name="route",
    )(xn2t, W["wpqt"], W["subk"])


def _peer_kernel(xn2t_ref, s_ref, stats_ref, ht_ref, u_ref, vt_ref, gfin_ref, y_ref,
                 e1_ref, e2_ref, acc_ref, hid_ref, a_ref):
    e = pl.program_id(2)
    ne = pl.num_programs(2)
    ec = u_ref.shape[0]
    tt = xn2t_ref.shape[2]
    rows_per_chunk = ec // N_KEYS

    @pl.when(e == 0)
    def _():
        for h in range(PEER_HEADS):
            amax = stats_ref[0, h * 4 + 1:h * 4 + 2, :]
            bmax = stats_ref[0, h * 4 + 2:h * 4 + 3, :]
            z = stats_ref[0, h * 4 + 3:h * 4 + 4, :]
            s1 = s_ref[0, (2 * h) * N_KEYS:(2 * h + 1) * N_KEYS, :]
            s2 = s_ref[0, (2 * h + 1) * N_KEYS:(2 * h + 2) * N_KEYS, :]
            e1_ref[h * N_KEYS:(h + 1) * N_KEYS, :] = jnp.exp(s1 - amax)
            e2_ref[h * N_KEYS:(h + 1) * N_KEYS, :] = jnp.exp(s2 - bmax) / z
        acc_ref[...] = jnp.zeros_like(acc_ref)

    def build(rs):
        first = [e * rows_per_chunk + r for r in rs]
        s1rows = [[s_ref[0, pl.ds(2 * h * N_KEYS + i, 1), :] for h in range(PEER_HEADS)] for i in first]
        e1rows = [[e1_ref[pl.ds(h * N_KEYS + i, 1), :] for h in range(PEER_HEADS)] for i in first]
        for t0 in range(0, tt, LANES):
            lanes = slice(t0, t0 + LANES)
            for j0 in range(0, N_KEYS, PEER_TJ):
                ws = [jnp.zeros((PEER_TJ, LANES), jnp.float32) for _ in rs]
                for h in range(PEER_HEADS):
                    tau = stats_ref[0, h * 4:h * 4 + 1, lanes]
                    s2 = s_ref[0, (2 * h + 1) * N_KEYS + j0:(2 * h + 1) * N_KEYS + j0 + PEER_TJ, lanes]
                    e2 = e2_ref[h * N_KEYS + j0:h * N_KEYS + j0 + PEER_TJ, lanes]
                    for k in range(len(rs)):
                        ws[k] = ws[k] + jnp.where(s1rows[k][h][:, lanes] + s2 >= tau,
                                                  e1rows[k][h][:, lanes] * e2, 0.0)
                for k, r in enumerate(rs):
                    rows = slice(r * N_KEYS + j0, r * N_KEYS + j0 + PEER_TJ)
                    hid = hid_ref[rows, lanes]
                    act = 0.5 * hid * (1.0 + lax.erf(hid * (2.0 ** -0.5)))
                    a_ref[rows, lanes] = (act * ws[k]).astype(MM_DTYPE)

    half = ec // 2
    rows_half = rows_per_chunk // 2
    xt = xn2t_ref[0]
    hid_ref[:half] = _dot(u_ref[:half], xt)
    hid_ref[half:] = _dot(u_ref[half:], xt)
    build(list(range(rows_half)))
    acc_ref[...] += _dot(vt_ref[:, :half], a_ref[:half])
    build(list(range(rows_half, rows_per_chunk)))
    acc_ref[...] += _dot(vt_ref[:, half:], a_ref[half:])

    @pl.when(e == ne - 1)
    def _():
        yt = ht_ref[0] + acc_ref[...]
        r = lax.rsqrt(jnp.mean(yt * yt, axis=0, keepdims=True) + EPS)
        y_ref[0] = (yt * r * gfin_ref[...]).T


def _peer(xn2t, s, stats, ht, W):
    B, _, S = xn2t.shape
    tt, ec = TT_PEER, EC_PEER
    return pl.pallas_call(
        _peer_kernel,
        grid=(B, S // tt, N_EXPERTS // ec),
        in_specs=[
            pl.BlockSpec((1, D_MODEL, tt), lambda b, i, e: (b, 0, i)),
            pl.BlockSpec((1, PEER_HEADS * 2 * N_KEYS, tt), lambda b, i, e: (b, 0, i)),
            pl.BlockSpec((1, PEER_HEADS * 4, tt), lambda b, i, e: (b, 0, i)),
            pl.BlockSpec((1, D_MODEL, tt), lambda b, i, e: (b, 0, i)),
            pl.BlockSpec((ec, D_MODEL), lambda b, i, e: (e, 0)),
            pl.BlockSpec((D_MODEL, ec), lambda b, i, e: (0, e)),
            pl.BlockSpec((D_MODEL, tt), lambda b, i, e: (0, 0)),
        ],
        out_specs=pl.BlockSpec((1, tt, D_MODEL), lambda b, i, e: (b, i, 0)),
        out_shape=jax.ShapeDtypeStruct((B, S, D_MODEL), jnp.float32),
        scratch_shapes=[
            pltpu.VMEM((PEER_HEADS * N_KEYS, tt), jnp.float32),
            pltpu.VMEM((PEER_HEADS * N_KEYS, tt), jnp.float32),
            pltpu.VMEM((D_MODEL, tt), jnp.float32),
            pltpu.VMEM((ec, tt), jnp.float32),
            pltpu.VMEM((ec, tt), MM_DTYPE),
        ],
        compiler_params=_cparams(("parallel", "parallel", "arbitrary")),
        name="peer",
    )(xn2t, s, stats, ht, W["u"], W["vt"], W["gfin"])


def _forward(x, W):
    qat, ka, vat, qan, kan, qbt, kb, vbt, qbn, kbn = _proj(x, W)
    oat = _attention(qat, ka, vat, qan, kan, groups_per_step=1, rep=H_A // G_A, dv=HD_A, name="attn_gqa")
    obt = _attention(qbt, kb, vbt, qbn, kbn, groups_per_step=2, rep=1, dv=V_B, name="attn_mla")
    ht, xn2t = _post(x, oat, obt, W)
    s, stats = _route(xn2t, W)
    return _peer(xn2t, s, stats, ht, W)


def kernel(x_prompt, x_sample, ln1_g, w_in, qa_norm_g, ka_norm_g, cq_norm_g, ckv_norm_g, w_uq, w_ukv,
           w_br_a, w_br_b, w_out, ln2_g, w_pq, sub_keys, expert_u, expert_v, final_g):
    params = (ln1_g, w_in, qa_norm_g, ka_norm_g, cq_norm_g, ckv_norm_g, w_uq, w_ukv,
              w_br_a, w_br_b, w_out, ln2_g, w_pq, sub_keys, expert_u, expert_v, final_g)
    assert x_prompt.shape[1] == x_sample.shape[1]
    W = _prepare(x_prompt.shape[1], *params)
    return (_forward(x_prompt, W), _forward(x_sample, W))
```

```python
import functools
import math

import numpy as np
import jax
import jax.numpy as jnp
from jax import lax
from jax.experimental import pallas as pl
from jax.experimental.pallas import tpu as pltpu

D_MODEL = 1024
GRID_W = 64
ROPE_THETA = 10000.0
EPS = 1e-6
H_A, G_A, HD_A = 8, 2, 64
H_B, Q_LORA, KV_LORA, NOPE_B, ROPE_B, V_B = 8, 256, 128, 64, 32, 64
PEER_HEADS, N_KEYS, D_KEY, PEER_TOPK = 8, 128, 256, 16
N_EXPERTS = N_KEYS * N_KEYS
HALF_KEY = D_KEY // 2

LANES = 128
SUBLANES = 8
VMEM_LIMIT_BYTES = 56 * 1024 * 1024

DQ_PAD = LANES
LOG2E = math.log2(math.e)
MM_DTYPE = jnp.bfloat16
ONES_ROWS = 2 * SUBLANES
VROWS_A = HD_A + ONES_ROWS
VROWS_B = V_B + ONES_ROWS
SAFE_SCORE_BOUND = 60.0

TM_PROJ = 512
TQ_ATTN = 512
TK_ATTN = 256
ATTN_UNROLL = 4
TM_POST = 512
TR_ROUTE = SUBLANES * LANES
TT_PEER = 512
EC_PEER = 1024
PEER_TJ = 32

_R_QA, _R_QA2, _R_KA, _R_KA2, _R_VA, _R_CQ, _R_CKV, _R_KR, _R_KR2, _R_END = (
    0, 512, 1024, 1152, 1280, 1408, 1664, 1792, 1824, 1856)
_T_QAC, _T_QAS, _T_KAC, _T_KAS, _T_QBC, _T_QBS, _T_KBC, _T_KBS, _T_END = (
    0, 64, 128, 192, 256, 288, 320, 352, 384)


def _cparams(sem, flags=None):
    return pltpu.CompilerParams(dimension_semantics=sem, vmem_limit_bytes=VMEM_LIMIT_BYTES, flags=flags)


def _dot(a, b):
    return jnp.dot(a, b, preferred_element_type=jnp.float32)


def _rope_partner(d):
    half, qtr = d // 2, d // 4
    e = np.arange(d)
    within = e % half
    first = within < qtr
    partner = np.where(first, e + qtr, e - qtr)
    sign = np.where(first, -1.0, 1.0).astype(np.float32)
    freq_idx = within % qtr
    use_col = e >= half
    return partner, sign, freq_idx, use_col


def _rope_tables(S, d):
    half = d // 2
    partner, sign, freq_idx, use_col = _rope_partner(d)
    t = jnp.arange(S, dtype=jnp.int32)
    r = (t // GRID_W).astype(jnp.float32)
    c = (t % GRID_W).astype(jnp.float32)
    freqs = ROPE_THETA ** (-jnp.arange(0, half, 2, dtype=jnp.float32) / half)
    f = freqs[freq_idx]
    pos = jnp.where(jnp.asarray(use_col)[:, None], c[None, :], r[None, :])
    ang = pos * f[:, None]
    return jnp.cos(ang), jnp.sin(ang) * jnp.asarray(sign)[:, None], partner


def _prepare(S, ln1_g, w_in, qa_norm_g, ka_norm_g, cq_norm_g, ckv_norm_g, w_uq, w_ukv,
             w_br_a, w_br_b, w_out, ln2_g, w_pq, sub_keys, expert_u, expert_v, final_g):
    bf = MM_DTYPE
    w = w_in[0]
    c_qa, c_ka, c_va, c_cq, c_ckv, c_kr, c_ga = np.cumsum([512, 128, 128, 256, 128, 32, 1024]).tolist()
    p64, _, _, _ = _rope_partner(HD_A)
    p32, _, _, _ = _rope_partner(ROPE_B)
    qa_perm = (np.arange(H_A)[:, None] * HD_A + p64[None, :]).reshape(-1)
    ka_perm = (np.arange(G_A)[:, None] * HD_A + p64[None, :]).reshape(-1)
    w_qa, w_ka = w[:, :c_qa], w[:, c_qa:c_ka]
    w_kr = w[:, c_ckv:c_kr]
    w1 = jnp.concatenate([
        w_qa, w_qa[:, qa_perm], w_ka, w_ka[:, ka_perm], w[:, c_ka:c_va],
        w[:, c_va:c_cq], w[:, c_cq:c_ckv], w_kr, w_kr[:, p32]], axis=1)
    w1t = w1.T.astype(bf)
    wgt = w[:, c_kr:].T.astype(bf)

    uq = w_uq[0].reshape(Q_LORA, H_B, NOPE_B + ROPE_B)
    uq_nope = uq[:, :, :NOPE_B].reshape(Q_LORA, H_B * NOPE_B)
    uq_rope = uq[:, :, NOPE_B:]
    wuqt = jnp.concatenate([uq_nope, uq_rope.reshape(Q_LORA, -1),
                            uq_rope[:, :, p32].reshape(Q_LORA, -1)], axis=1).T.astype(bf)
    ukv = w_ukv[0].reshape(KV_LORA, H_B, NOPE_B + V_B)
    wukvt = jnp.concatenate([ukv[:, :, :NOPE_B].reshape(KV_LORA, -1),
                             ukv[:, :, NOPE_B:].reshape(KV_LORA, -1)], axis=1).T.astype(bf)

    cos64, sin64, _ = _rope_tables(S, HD_A)
    cos32, sin32, _ = _rope_tables(S, ROPE_B)
    sa = (HD_A ** -0.5) * LOG2E
    sb = ((NOPE_B + ROPE_B) ** -0.5) * LOG2E
    gq, gk = qa_norm_g[0], ka_norm_g[0]
    tabs = jnp.concatenate([
        cos64 * (gq * sa)[:, None], sin64 * (gq[p64] * sa)[:, None],
        cos64 * gk[:, None], sin64 * gk[p64][:, None],
        cos32 * sb, sin32 * sb, cos32, sin32], axis=0)

    def bcast(g, n):
        return jnp.broadcast_to(g.reshape(-1, 1), (g.size, n))

    return dict(
        ln1=ln1_g[0].reshape(1, D_MODEL), w1t=w1t, wgt=wgt, wuqt=wuqt, wukvt=wukvt, tabs=tabs,
        gcq=bcast(cq_norm_g[0], TM_PROJ), gckv=bcast(ckv_norm_g[0], TM_PROJ),
        wbrat=w_br_a[0].T.astype(bf), wbrbt=w_br_b[0].T.astype(bf), woutt=w_out[0].T.astype(bf),
        g2=bcast(ln2_g[0], TM_POST), wpqt=w_pq[0].T.astype(bf),
        subk=sub_keys[0].reshape(PEER_HEADS * 2, N_KEYS, HALF_KEY).astype(bf),
        u=expert_u[0].astype(bf), vt=expert_v[0].T.astype(bf),
        gfin=bcast(final_g, TT_PEER), sb=sb)


def _proj_kernel(x_ref, ln1_ref, w1t_ref, tab_ref, gcq_ref, gckv_ref, wuqt_ref, wukvt_ref,
                 qat_ref, ka_ref, vat_ref, qan_ref, kan_ref, qbt_ref, kb_ref, vbt_ref, qbn_ref, kbn_ref, *, sb):
    bf = MM_DTYPE
    x = x_ref[0]
    tm = x.shape[0]
    xn = x * lax.rsqrt(jnp.mean(x * x, axis=-1, keepdims=True) + EPS) * ln1_ref[...]
    xnt = xn.T.astype(bf)
    p = _dot(w1t_ref[...], xnt)
    tab = tab_ref[...]
    zeros64 = jnp.zeros((64, tm), jnp.float32)
    zeros32 = jnp.zeros((32, tm), jnp.float32)
    ones = jnp.ones((ONES_ROWS, tm), bf)

    def rstd(y):
        return lax.rsqrt(jnp.mean(y * y, axis=0, keepdims=True) + EPS)

    def sqnorm(yb):
        yf = yb.astype(jnp.float32)
        return jnp.sum(yf * yf, axis=0, keepdims=True)

    rep = H_A // G_A
    for h in range(H_A):
        y = p[_R_QA + h * HD_A:_R_QA + (h + 1) * HD_A]
        y2 = p[_R_QA2 + h * HD_A:_R_QA2 + (h + 1) * HD_A]
        q = ((y * tab[_T_QAC:_T_QAS] + y2 * tab[_T_QAS:_T_KAC]) * rstd(y)).astype(bf)
        qat_ref[0, h * DQ_PAD:h * DQ_PAD + HD_A, :] = q
        qat_ref[0, h * DQ_PAD + HD_A:(h + 1) * DQ_PAD, :] = zeros64.astype(bf)
        qan_ref[0, h // rep, h % rep:h % rep + 1, :] = sqnorm(q)
    for g in range(G_A):
        y = p[_R_KA + g * HD_A:_R_KA + (g + 1) * HD_A]
        y2 = p[_R_KA2 + g * HD_A:_R_KA2 + (g + 1) * HD_A]
        k = (y * tab[_T_KAC:_T_KAS] + y2 * tab[_T_KAS:_T_QBC]) * rstd(y)
        ka_ref[0, g] = jnp.concatenate([k, zeros64], axis=0).T.astype(bf)
        kan_ref[0, g, 0:1, :] = sqnorm(k.astype(bf))
        vat_ref[0, g * VROWS_A:g * VROWS_A + HD_A, :] = p[_R_VA + g * HD_A:_R_VA + (g + 1) * HD_A].astype(bf)
        vat_ref[0, g * VROWS_A + HD_A:(g + 1) * VROWS_A, :] = ones

    cq = p[_R_CQ:_R_CKV]
    cqn = (cq * rstd(cq) * gcq_ref[...]).astype(bf)
    q2 = _dot(wuqt_ref[...], cqn)
    for h in range(H_B):
        nope = (q2[h * NOPE_B:(h + 1) * NOPE_B] * sb).astype(bf)
        yr = q2[512 + h * ROPE_B:512 + (h + 1) * ROPE_B]
        yr2 = q2[768 + h * ROPE_B:768 + (h + 1) * ROPE_B]
        rope = (yr * tab[_T_QBC:_T_QBS] + yr2 * tab[_T_QBS:_T_KBC]).astype(bf)
        base = h * DQ_PAD
        qbt_ref[0, base:base + NOPE_B, :] = nope
        qbt_ref[0, base + NOPE_B:base + NOPE_B + ROPE_B, :] = rope
        qbt_ref[0, base + NOPE_B + ROPE_B:base + DQ_PAD, :] = zeros32.astype(bf)
        qbn_ref[0, h // 2, h % 2:h % 2 + 1, :] = sqnorm(nope) + sqnorm(rope)
    ckv = p[_R_CKV:_R_KR]
    ckvn = (ckv * rstd(ckv) * gckv_ref[...]).astype(bf)
    kv = _dot(wukvt_ref[...], ckvn)
    krope = p[_R_KR:_R_KR2] * tab[_T_KBC:_T_KBS] + p[_R_KR2:_R_END] * tab[_T_KBS:_T_END]
    krope_n = sqnorm(krope.astype(bf))
    for h in range(H_B):
        knope = kv[h * NOPE_B:(h + 1) * NOPE_B]
        kt = jnp.concatenate([knope, krope, zeros32], axis=0)
        kb_ref[0, h] = kt.T.astype(bf)
        kbn_ref[0, h // 2, h % 2:h % 2 + 1, :] = sqnorm(knope.astype(bf)) + krope_n
        v0 = H_B * NOPE_B + h * V_B
        vbt_ref[0, h * VROWS_B:h * VROWS_B + V_B, :] = kv[v0:v0 + V_B].astype(bf)
        vbt_ref[0, h * VROWS_B + V_B:(h + 1) * VROWS_B, :] = ones


def _proj(x, W):
    B, S, _ = x.shape
    tm = TM_PROJ
    bf = MM_DTYPE
    f32 = jnp.float32
    rep = H_A // G_A
    const = lambda shape: pl.BlockSpec(shape, lambda b, i: (0,) * len(shape))
    return pl.pallas_call(
        functools.partial(_proj_kernel, sb=W["sb"]),
        grid=(B, S // tm),
        in_specs=[
            pl.BlockSpec((1, tm, D_MODEL), lambda b, i: (b, i, 0)),
            const((1, D_MODEL)), const((_R_END, D_MODEL)),
            pl.BlockSpec((_T_END, tm), lambda b, i: (0, i)),
            const((Q_LORA, tm)), const((KV_LORA, tm)),
            const((1024, Q_LORA)), const((1024, KV_LORA)),
        ],
        out_specs=[
            pl.BlockSpec((1, H_A * DQ_PAD, tm), lambda b, i: (b, 0, i)),
            pl.BlockSpec((1, G_A, tm, DQ_PAD), lambda b, i: (b, 0, i, 0)),
            pl.BlockSpec((1, G_A * VROWS_A, tm), lambda b, i: (b, 0, i)),
            pl.BlockSpec((1, G_A, rep, tm), lambda b, i: (b, 0, 0, i)),
            pl.BlockSpec((1, G_A, 1, tm), lambda b, i: (b, 0, 0, i)),
            pl.BlockSpec((1, H_B * DQ_PAD, tm), lambda b, i: (b, 0, i)),
            pl.BlockSpec((1, H_B, tm, DQ_PAD), lambda b, i: (b, 0, i, 0)),
            pl.BlockSpec((1, H_B * VROWS_B, tm), lambda b, i: (b, 0, i)),
            pl.BlockSpec((1, H_B // 2, 2, tm), lambda b, i: (b, 0, 0, i)),
            pl.BlockSpec((1, H_B // 2, 2, tm), lambda b, i: (b, 0, 0, i)),
        ],
        out_shape=[
            jax.ShapeDtypeStruct((B, H_A * DQ_PAD, S), bf),
            jax.ShapeDtypeStruct((B, G_A, S, DQ_PAD), bf),
            jax.ShapeDtypeStruct((B, G_A * VROWS_A, S), bf),
            jax.ShapeDtypeStruct((B, G_A, rep, S), f32),
            jax.ShapeDtypeStruct((B, G_A, 1, S), f32),
            jax.ShapeDtypeStruct((B, H_B * DQ_PAD, S), bf),
            jax.ShapeDtypeStruct((B, H_B, S, DQ_PAD), bf),
            jax.ShapeDtypeStruct((B, H_B * VROWS_B, S), bf),
            jax.ShapeDtypeStruct((B, H_B // 2, 2, S), f32),
            jax.ShapeDtypeStruct((B, H_B // 2, 2, S), f32),
        ],
        compiler_params=_cparams(("parallel", "parallel")),
        name="proj",
    )(x, W["ln1"], W["w1t"], W["tabs"], W["gcq"], W["gckv"], W["wuqt"], W["wukvt"])


def _attn_kernel(qt_ref, k_ref, vt_ref, qn_ref, kn_ref, ot_ref, s_ref, p_ref, *, heads, rep, dv, tk, heads_per_loop):
    S = k_ref.shape[2]
    tq = qt_ref.shape[2]
    nk = S // tk
    vrows = dv + ONES_ROWS

    def scores(h, c):
        off = pl.multiple_of(c * tk, tk)
        kc = k_ref[0, h // rep, pl.ds(off, tk), :]
        return _dot(kc, qt_ref[0, h * DQ_PAD:(h + 1) * DQ_PAD, :])

    def vchunk(h, c):
        off = pl.multiple_of(c * tk, tk)
        g = h // rep
        return vt_ref[0, g * vrows:(g + 1) * vrows, pl.ds(off, tk)]

    def finish(h, acc):
        ot_ref[0, h * dv:(h + 1) * dv, :] = (acc[:dv] / acc[dv:dv + 1]).astype(ot_ref.dtype)

    zero_acc = lambda: jnp.zeros((vrows, tq), jnp.float32)
    bound2 = jnp.max(qn_ref[0, 0]) * jnp.max(kn_ref[0, 0])
    safe = bound2 <= SAFE_SCORE_BOUND * SAFE_SCORE_BOUND

    def probs(s, m, with_max):
        if not with_max:
            return m, None, jnp.exp2(s).astype(MM_DTYPE)
        mn = jnp.maximum(m, jnp.max(s, axis=0, keepdims=True))
        return mn, jnp.exp2(m - mn), jnp.exp2(s - mn).astype(MM_DTYPE)

    def accumulate(h, c, p_slot, alpha, acc):
        pv = _dot(vchunk(h, c), p_slot[...])
        return pv + (acc if alpha is None else alpha * acc)

    def run(with_max):
        for h0 in range(0, heads, heads_per_loop):
            hs = list(range(h0, h0 + heads_per_loop))
            for i, h in enumerate(hs):
                s_ref[0, i] = scores(h, 0)
                p_ref[1, i] = jnp.zeros((tk, tq), MM_DTYPE)

            def body(j, carry, hs=hs):
                c0 = 2 * j
                prev = jnp.maximum(c0 - 1, 0)
                nxt = jnp.minimum(c0 + 2, nk - 1)
                out = []
                for i, (h, (m, alpha_b, acc)) in enumerate(zip(hs, carry)):
                    s_ref[1, i] = scores(h, c0 + 1)
                    acc = accumulate(h, prev, p_ref.at[1, i], alpha_b, acc)
                    m, alpha_a, pa = probs(s_ref[0, i], m, with_max)
                    p_ref[0, i] = pa
                    acc = accumulate(h, c0, p_ref.at[0, i], alpha_a, acc)
                    s_ref[0, i] = scores(h, nxt)
                    m, alpha_b, pb = probs(s_ref[1, i], m, with_max)
                    p_ref[1, i] = pb
                    out.append((m, alpha_b, acc))
                return tuple(out)

            one = jnp.ones((1, tq), jnp.float32) if with_max else None
            init = tuple((jnp.full((1, tq), -jnp.inf, jnp.float32), one, zero_acc()) for _ in hs)
            res = lax.fori_loop(0, nk // 2, body, init, unroll=ATTN_UNROLL)
            for i, ((m, alpha_b, acc), h) in enumerate(zip(res, hs)):
                finish(h, accumulate(h, nk - 1, p_ref.at[1, i], alpha_b, acc))

    pl.when(safe)(lambda: run(False))
    pl.when(jnp.logical_not(safe))(lambda: run(True))


def _attention(qt, k, vt, qn, kn, *, groups_per_step, rep, dv, name):
    B, _, S = qt.shape
    G = k.shape[1]
    gs = groups_per_step
    heads = gs * rep
    tq = TQ_ATTN
    hpl = min(2, heads)
    vrows = dv + ONES_ROWS
    assert (S // TK_ATTN) % 2 == 0
    kern = functools.partial(_attn_kernel, heads=heads, rep=rep, dv=dv, tk=TK_ATTN, heads_per_loop=hpl)
    return pl.pallas_call(
        kern,
        grid=(B, G // gs, S // tq),
        in_specs=[
            pl.BlockSpec((1, heads * DQ_PAD, tq), lambda b, g, i: (b, g, i)),
            pl.BlockSpec((1, gs, S, DQ_PAD), lambda b, g, i: (b, g, 0, 0)),
            pl.BlockSpec((1, gs * vrows, S), lambda b, g, i: (b, g, 0)),
            pl.BlockSpec((1, 1, heads, tq), lambda b, g, i: (b, g, 0, i)),
            pl.BlockSpec((1, 1, gs, S), lambda b, g, i: (b, g, 0, 0)),
        ],
        out_specs=pl.BlockSpec((1, heads * dv, tq), lambda b, g, i: (b, g, i)),
        out_shape=jax.ShapeDtypeStruct((B, G * rep * dv, S), MM_DTYPE),
        scratch_shapes=[pltpu.VMEM((2, hpl, TK_ATTN, tq), jnp.float32),
                        pltpu.VMEM((2, hpl, TK_ATTN, tq), MM_DTYPE)],
        compiler_params=_cparams(("parallel", "parallel", "arbitrary")),
        name=name,
    )(qt, k, vt, qn, kn)


def _post_kernel(x_ref, ln1_ref, wgt_ref, oat_ref, obt_ref, wbrat_ref, wbrbt_ref, woutt_ref, g2_ref,
                 ht_ref, xn2t_ref):
    bf = MM_DTYPE
    x = x_ref[0]
    xn = x * lax.rsqrt(jnp.mean(x * x, axis=-1, keepdims=True) + EPS) * ln1_ref[...]
    xnt = xn.T.astype(bf)
    gates = jax.nn.sigmoid(_dot(wgt_ref[...], xnt))
    ma = _dot(wbrat_ref[...], oat_ref[0])
    mb = _dot(wbrbt_ref[...], obt_ref[0])
    merged = gates[:D_MODEL] * ma + gates[D_MODEL:] * mb
    ht = x.T + _dot(woutt_ref[...], merged.astype(bf))
    ht_ref[0] = ht
    r2 = lax.rsqrt(jnp.mean(ht * ht, axis=0, keepdims=True) + EPS)
    xn2t_ref[0] = (ht * r2 * g2_ref[...]).astype(bf)


def _post(x, oat, obt, W):
    B, S, _ = x.shape
    tm = TM_POST
    const = lambda shape: pl.BlockSpec(shape, lambda b, i: (0,) * len(shape))
    return pl.pallas_call(
        _post_kernel,
        grid=(B, S // tm),
        in_specs=[
            pl.BlockSpec((1, tm, D_MODEL), lambda b, i: (b, i, 0)),
            const((1, D_MODEL)), const((2 * D_MODEL, D_MODEL)),
            pl.BlockSpec((1, H_A * HD_A, tm), lambda b, i: (b, 0, i)),
            pl.BlockSpec((1, H_B * V_B, tm), lambda b, i: (b, 0, i)),
            const((D_MODEL, H_A * HD_A)), const((D_MODEL, H_B * V_B)), const((D_MODEL, D_MODEL)),
            const((D_MODEL, tm)),
        ],
        out_specs=[
            pl.BlockSpec((1, D_MODEL, tm), lambda b, i: (b, 0, i)),
            pl.BlockSpec((1, D_MODEL, tm), lambda b, i: (b, 0, i)),
        ],
        out_shape=[
            jax.ShapeDtypeStruct((B, D_MODEL, S), jnp.float32),
            jax.ShapeDtypeStruct((B, D_MODEL, S), MM_DTYPE),
        ],
        compiler_params=_cparams(("parallel", "parallel")),
        name="post",
    )(x, W["ln1"], W["wgt"], oat, obt, W["wbrat"], W["wbrbt"], W["woutt"], W["g2"])


def _cmpx(xs, i, j):
    a, b = xs[i], xs[j]
    xs[i], xs[j] = jnp.maximum(a, b), jnp.minimum(a, b)


def _bitonic_merge_desc(xs):
    n = len(xs)
    j = n // 2
    while j >= 1:
        for i in range(n):
            l = i ^ j
            if l > i:
                _cmpx(xs, i, l)
        j //= 2


def _bitonic_sort_desc(xs):
    n = len(xs)
    k = 2
    while k <= n:
        j = k // 2
        while j >= 1:
            for i in range(n):
                l = i ^ j
                if l > i:
                    if (i & k) == 0:
                        _cmpx(xs, i, l)
                    else:
                        _cmpx(xs, l, i)
            j //= 2
        k *= 2


_CAND_PAIRS = [(i, j) for i in range(PEER_TOPK) for j in range(PEER_TOPK) if (i + 1) * (j + 1) <= PEER_TOPK]


def _route_kernel(xn2t_ref, wpqt_ref, subk_ref, s_ref, stats_ref):
    bf = MM_DTYPE
    tr = xn2t_ref.shape[2]
    groups = tr // LANES
    qt = _dot(wpqt_ref[...], xn2t_ref[0]).astype(bf)
    row = lax.broadcasted_iota(jnp.int32, (SUBLANES, LANES), 0)
    nblk = N_KEYS // SUBLANES
    for h in range(PEER_HEADS):
        tops = []
        for pp in range(2):
            hp = h * 2 + pp
            st = _dot(subk_ref[hp], qt[hp * HALF_KEY:(hp + 1) * HALF_KEY])
            s_ref[0, hp * N_KEYS:(hp + 1) * N_KEYS, :] = st
            xs = [st[v * SUBLANES:(v + 1) * SUBLANES] for v in range(nblk)]
            _bitonic_sort_desc(xs)
            for shift in (4, 2, 1):
                rolled = [pltpu.roll(xs[nblk - 1 - v], shift, 0) for v in range(nblk)]
                xs = [jnp.maximum(a, b) for a, b in zip(xs, rolled)]
                _bitonic_merge_desc(xs)
            comp = []
            for v in range(PEER_TOPK):
                c = xs[v][:, 0:LANES]
                for gidx in range(1, groups):
                    c = jnp.where(row == gidx, xs[v][:, gidx * LANES:(gidx + 1) * LANES], c)
                comp.append(c)
            tops.append(comp)
        a, b = tops
        cands = [a[i] + b[j] for (i, j) in _CAND_PAIRS]
        cands += [jnp.full((SUBLANES, LANES), -jnp.inf, jnp.float32)] * (64 - len(cands))
        _bitonic_sort_desc(cands)
        cmax = cands[0]
        z = jnp.zeros((SUBLANES, LANES), jnp.float32)
        for kk in range(PEER_TOPK):
            z = z + jnp.exp(cands[kk] - cmax)
        for r, val in enumerate((cands[PEER_TOPK - 1], a[0], b[0], z)):
            wide = jnp.concatenate([val[gidx:gidx + 1, :] for gidx in range(groups)], axis=1)
            stats_ref[0, h * 4 + r:h * 4 + r + 1, :] = wide


def _route(xn2t, W):
    B, _, S = xn2t.shape
    tr = TR_ROUTE
    const = lambda shape: pl.BlockSpec(shape, lambda b, i: (0,) * len(shape))
    return pl.pallas_call(
        _route_kernel,
        grid=(B, S // tr),
        in_specs=[
            pl.BlockSpec((1, D_MODEL, tr), lambda b, i: (b, 0, i)),
            const((PEER_HEADS * D_KEY, D_MODEL)),
            const((PEER_HEADS * 2, N_KEYS, HALF_KEY)),
        ],
        out_specs=[
            pl.BlockSpec((1, PEER_HEADS * 2 * N_KEYS, tr), lambda b, i: (b, 0, i)),
            pl.BlockSpec((1, PEER_HEADS * 4, tr), lambda b, i: (b, 0, i)),
        ],
        out_shape=[
            jax.ShapeDtypeStruct((B, PEER_HEADS * 2 * N_KEYS, S), jnp.float32),
            jax.ShapeDtypeStruct((B, PEER_HEADS * 4, S), jnp.float32),
        ],
        compiler_params=_cparams(("parallel", "parallel")),
        name="route",
    )(xn2t, W["wpqt"], W["subk"])


def _peer_kernel(xn2t_ref, s_ref, stats_ref, ht_ref, u_ref, vt_ref, gfin_ref, y_ref,
                 e1_ref, e2_ref, acc_ref, hid_ref, a_ref, s1_ref, s2_ref, tau_ref):
    e = pl.program_id(2)
    ne = pl.num_programs(2)
    ec = u_ref.shape[0]
    tt = xn2t_ref.shape[2]
    rows_per_chunk = ec // N_KEYS

    @pl.when(e == 0)
    def _():
        for h in range(PEER_HEADS):
            amax = stats_ref[0, h * 4 + 1:h * 4 + 2, :]
            bmax = stats_ref[0, h * 4 + 2:h * 4 + 3, :]
            z = stats_ref[0, h * 4 + 3:h * 4 + 4, :]
            s1 = s_ref[0, (2 * h) * N_KEYS:(2 * h + 1) * N_KEYS, :]
            s2 = s_ref[0, (2 * h + 1) * N_KEYS:(2 * h + 2) * N_KEYS, :]
            e1_ref[h * N_KEYS:(h + 1) * N_KEYS, :] = jnp.exp(s1 - amax)
            e2_ref[h * N_KEYS:(h + 1) * N_KEYS, :] = jnp.exp(s2 - bmax) / z
            s1_ref[h * N_KEYS:(h + 1) * N_KEYS, :] = s1
            s2_ref[h * N_KEYS:(h + 1) * N_KEYS, :] = s2
            tau_ref[h:h + 1, :] = stats_ref[0, h * 4:h * 4 + 1, :]
        acc_ref[...] = jnp.zeros_like(acc_ref)

    def build(rs):
        first = [e * rows_per_chunk + r for r in rs]
        s1rows = [[s1_ref[pl.ds(h * N_KEYS + i, 1), :] for h in range(PEER_HEADS)] for i in first]
        e1rows = [[e1_ref[pl.ds(h * N_KEYS + i, 1), :] for h in range(PEER_HEADS)] for i in first]
        for t0 in range(0, tt, LANES):
            lanes = slice(t0, t0 + LANES)
            for j0 in range(0, N_KEYS, PEER_TJ):
                ws = [jnp.zeros((PEER_TJ, LANES), jnp.float32) for _ in rs]
                for h in range(PEER_HEADS):
                    tau = tau_ref[h:h + 1, lanes]
                    s2 = s2_ref[h * N_KEYS + j0:h * N_KEYS + j0 + PEER_TJ, lanes]
                    e2 = e2_ref[h * N_KEYS + j0:h * N_KEYS + j0 + PEER_TJ, lanes]
                    for k in range(len(rs)):
                        ws[k] = ws[k] + jnp.where(s1rows[k][h][:, lanes] + s2 >= tau,
                                                  e1rows[k][h][:, lanes] * e2, 0.0)
                for k, r in enumerate(rs):
                    rows = slice(r * N_KEYS + j0, r * N_KEYS + j0 + PEER_TJ)
                    hid = hid_ref[rows, lanes]
                    act = 0.5 * hid * (1.0 + lax.erf(hid * (2.0 ** -0.5)))
                    a_ref[rows, lanes] = (act * ws[k]).astype(MM_DTYPE)

    half = ec // 2
    rows_half = rows_per_chunk // 2
    xt = xn2t_ref[0]
    hid_ref[:half] = _dot(u_ref[:half], xt)
    hid_ref[half:] = _dot(u_ref[half:], xt)
    build(list(range(rows_half)))
    acc_ref[...] += _dot(vt_ref[:, :half], a_ref[:half])
    build(list(range(rows_half, rows_per_chunk)))
    acc_ref[...] += _dot(vt_ref[:, half:], a_ref[half:])

    @pl.when(e == ne - 1)
    def _():
        yt = ht_ref[0] + acc_ref[...]
        r = lax.rsqrt(jnp.mean(yt * yt, axis=0, keepdims=True) + EPS)
        y_ref[0] = (yt * r * gfin_ref[...]).T


def _peer(xn2t, s, stats, ht, W):
    B, _, S = xn2t.shape
    tt, ec = TT_PEER, EC_PEER
    return pl.pallas_call(
        _peer_kernel,
        grid=(B, S // tt, N_EXPERTS // ec),
        in_specs=[
            pl.BlockSpec((1, D_MODEL, tt), lambda b, i, e: (b, 0, i)),
            pl.BlockSpec((1, PEER_HEADS * 2 * N_KEYS, tt), lambda b, i, e: (b, 0, i)),
            pl.BlockSpec((1, PEER_HEADS * 4, tt), lambda b, i, e: (b, 0, i)),
            pl.BlockSpec((1, D_MODEL, tt), lambda b, i, e: (b, 0, i)),
            pl.BlockSpec((ec, D_MODEL), lambda b, i, e: (e, 0)),
            pl.BlockSpec((D_MODEL, ec), lambda b, i, e: (0, e)),
            pl.BlockSpec((D_MODEL, tt), lambda b, i, e: (0, 0)),
        ],
        out_specs=pl.BlockSpec((1, tt, D_MODEL), lambda b, i, e: (b, i, 0)),
        out_shape=jax.ShapeDtypeStruct((B, S, D_MODEL), jnp.float32),
        scratch_shapes=[
            pltpu.VMEM((PEER_HEADS * N_KEYS, tt), jnp.float32),
            pltpu.VMEM((PEER_HEADS * N_KEYS, tt), jnp.float32),
            pltpu.VMEM((D_MODEL, tt), jnp.float32),
            pltpu.VMEM((ec, tt), jnp.float32),
            pltpu.VMEM((ec, tt), MM_DTYPE),
            pltpu.VMEM((PEER_HEADS * N_KEYS, tt), jnp.float32),
            pltpu.VMEM((PEER_HEADS * N_KEYS, tt), jnp.float32),
            pltpu.VMEM((PEER_HEADS, tt), jnp.float32),
        ],
        compiler_params=_cparams(("parallel", "parallel", "arbitrary")),
        name="peer",
    )(xn2t, s, stats, ht, W["u"], W["vt"], W["gfin"])


def _forward(x, W):
    qat, ka, vat, qan, kan, qbt, kb, vbt, qbn, kbn = _proj(x, W)
    oat = _attention(qat, ka, vat, qan, kan, groups_per_step=1, rep=H_A // G_A, dv=HD_A, name="attn_gqa")
    obt = _attention(qbt, kb, vbt, qbn, kbn, groups_per_step=2, rep=1, dv=V_B, name="attn_mla")
    ht, xn2t = _post(x, oat, obt, W)
    s, stats = _route(xn2t, W)
    return _peer(xn2t, s, stats, ht, W)


def kernel(x_prompt, x_sample, ln1_g, w_in, qa_norm_g, ka_norm_g, cq_norm_g, ckv_norm_g, w_uq, w_ukv,
           w_br_a, w_br_b, w_out, ln2_g, w_pq, sub_keys, expert_u, expert_v, final_g):
    params = (ln1_g, w_in, qa_norm_g, ka_norm_g, cq_norm_g, ckv_norm_g, w_uq, w_ukv,
              w_br_a, w_br_b, w_out, ln2_g, w_pq, sub_keys, expert_u, expert_v, final_g)
    assert x_prompt.shape[1] == x_sample.shape[1]
    W = _prepare(x_prompt.shape[1], *params)
    return (_forward(x_prompt, W), _forward(x_sample, W))
```

```python
import functools
import math

import numpy as np
import jax
import jax.numpy as jnp
from jax import lax
from jax.experimental import pallas as pl
from jax.experimental.pallas import tpu as pltpu

D_MODEL = 1024
GRID_W = 64
ROPE_THETA = 10000.0
EPS = 1e-6
H_A, G_A, HD_A = 8, 2, 64
H_B, Q_LORA, KV_LORA, NOPE_B, ROPE_B, V_B = 8, 256, 128, 64, 32, 64
PEER_HEADS, N_KEYS, D_KEY, PEER_TOPK = 8, 128, 256, 16
N_EXPERTS = N_KEYS * N_KEYS
HALF_KEY = D_KEY // 2

LANES = 128
SUBLANES = 8
VMEM_LIMIT_BYTES = 56 * 1024 * 1024

DQ_PAD = LANES
LOG2E = math.log2(math.e)
MM_DTYPE = jnp.bfloat16
ONES_ROWS = 2 * SUBLANES
VROWS_A = HD_A + ONES_ROWS
VROWS_B = V_B + ONES_ROWS
SAFE_SCORE_BOUND = 60.0

TM_PROJ = 512
TQ_ATTN = 512
TK_ATTN = 256
ATTN_UNROLL = 8
TM_POST = 512
TR_ROUTE = SUBLANES * LANES
TT_PEER = 512
EC_PEER = 1024
PEER_TJ = 32

_R_QA, _R_QA2, _R_KA, _R_KA2, _R_VA, _R_CQ, _R_CKV, _R_KR, _R_KR2, _R_END = (
    0, 512, 1024, 1152, 1280, 1408, 1664, 1792, 1824, 1856)
_T_QAC, _T_QAS, _T_KAC, _T_KAS, _T_QBC, _T_QBS, _T_KBC, _T_KBS, _T_END = (
    0, 64, 128, 192, 256, 288, 320, 352, 384)


def _cparams(sem, flags=None):
    return pltpu.CompilerParams(dimension_semantics=sem, vmem_limit_bytes=VMEM_LIMIT_BYTES, flags=flags)


def _dot(a, b):
    return jnp.dot(a, b, preferred_element_type=jnp.float32)


def _rope_partner(d):
    half, qtr = d // 2, d // 4
    e = np.arange(d)
    within = e % half
    first = within < qtr
    partner = np.where(first, e + qtr, e - qtr)
    sign = np.where(first, -1.0, 1.0).astype(np.float32)
    freq_idx = within % qtr
    use_col = e >= half
    return partner, sign, freq_idx, use_col


def _rope_tables(S, d):
    half = d // 2
    partner, sign, freq_idx, use_col = _rope_partner(d)
    t = jnp.arange(S, dtype=jnp.int32)
    r = (t // GRID_W).astype(jnp.float32)
    c = (t % GRID_W).astype(jnp.float32)
    freqs = ROPE_THETA ** (-jnp.arange(0, half, 2, dtype=jnp.float32) / half)
    f = freqs[freq_idx]
    pos = jnp.where(jnp.asarray(use_col)[:, None], c[None, :], r[None, :])
    ang = pos * f[:, None]
    return jnp.cos(ang), jnp.sin(ang) * jnp.asarray(sign)[:, None], partner


def _prepare(S, ln1_g, w_in, qa_norm_g, ka_norm_g, cq_norm_g, ckv_norm_g, w_uq, w_ukv,
             w_br_a, w_br_b, w_out, ln2_g, w_pq, sub_keys, expert_u, expert_v, final_g):
    bf = MM_DTYPE
    w = w_in[0]
    c_qa, c_ka, c_va, c_cq, c_ckv, c_kr, c_ga = np.cumsum([512, 128, 128, 256, 128, 32, 1024]).tolist()
    p64, _, _, _ = _rope_partner(HD_A)
    p32, _, _, _ = _rope_partner(ROPE_B)
    qa_perm = (np.arange(H_A)[:, None] * HD_A + p64[None, :]).reshape(-1)
    ka_perm = (np.arange(G_A)[:, None] * HD_A + p64[None, :]).reshape(-1)
    w_qa, w_ka = w[:, :c_qa], w[:, c_qa:c_ka]
    w_kr = w[:, c_ckv:c_kr]
    w1 = jnp.concatenate([
        w_qa, w_qa[:, qa_perm], w_ka, w_ka[:, ka_perm], w[:, c_ka:c_va],
        w[:, c_va:c_cq], w[:, c_cq:c_ckv], w_kr, w_kr[:, p32]], axis=1)
    w1t = w1.T.astype(bf)
    wgt = w[:, c_kr:].T.astype(bf)

    uq = w_uq[0].reshape(Q_LORA, H_B, NOPE_B + ROPE_B)
    uq_nope = uq[:, :, :NOPE_B].reshape(Q_LORA, H_B * NOPE_B)
    uq_rope = uq[:, :, NOPE_B:]
    wuqt = jnp.concatenate([uq_nope, uq_rope.reshape(Q_LORA, -1),
                            uq_rope[:, :, p32].reshape(Q_LORA, -1)], axis=1).T.astype(bf)
    ukv = w_ukv[0].reshape(KV_LORA, H_B, NOPE_B + V_B)
    wukvt = jnp.concatenate([ukv[:, :, :NOPE_B].reshape(KV_LORA, -1),
                             ukv[:, :, NOPE_B:].reshape(KV_LORA, -1)], axis=1).T.astype(bf)

    cos64, sin64, _ = _rope_tables(S, HD_A)
    cos32, sin32, _ = _rope_tables(S, ROPE_B)
    sa = (HD_A ** -0.5) * LOG2E
    sb = ((NOPE_B + ROPE_B) ** -0.5) * LOG2E
    gq, gk = qa_norm_g[0], ka_norm_g[0]
    tabs = jnp.concatenate([
        cos64 * (gq * sa)[:, None], sin64 * (gq[p64] * sa)[:, None],
        cos64 * gk[:, None], sin64 * gk[p64][:, None],
        cos32 * sb, sin32 * sb, cos32, sin32], axis=0)

    def bcast(g, n):
        return jnp.broadcast_to(g.reshape(-1, 1), (g.size, n))

    return dict(
        ln1=ln1_g[0].reshape(1, D_MODEL), w1t=w1t, wgt=wgt, wuqt=wuqt, wukvt=wukvt, tabs=tabs,
        gcq=bcast(cq_norm_g[0], TM_PROJ), gckv=bcast(ckv_norm_g[0], TM_PROJ),
        wbrat=w_br_a[0].T.astype(bf), wbrbt=w_br_b[0].T.astype(bf), woutt=w_out[0].T.astype(bf),
        g2=bcast(ln2_g[0], TM_POST), wpqt=w_pq[0].T.astype(bf),
        subk=sub_keys[0].reshape(PEER_HEADS * 2, N_KEYS, HALF_KEY).astype(bf),
        u=expert_u[0].astype(bf), vt=expert_v[0].T.astype(bf),
        gfin=bcast(final_g, TT_PEER), sb=sb)


def _proj_kernel(x_ref, ln1_ref, w1t_ref, tab_ref, gcq_ref, gckv_ref, wuqt_ref, wukvt_ref,
                 qat_ref, ka_ref, vat_ref, qan_ref, kan_ref, qbt_ref, kb_ref, vbt_ref, qbn_ref, kbn_ref, *, sb):
    bf = MM_DTYPE
    x = x_ref[0]
    tm = x.shape[0]
    xn = x * lax.rsqrt(jnp.mean(x * x, axis=-1, keepdims=True) + EPS) * ln1_ref[...]
    xnt = xn.T.astype(bf)
    p = _dot(w1t_ref[...], xnt)
    tab = tab_ref[...]
    zeros64 = jnp.zeros((64, tm), jnp.float32)
    zeros32 = jnp.zeros((32, tm), jnp.float32)
    ones = jnp.ones((ONES_ROWS, tm), bf)

    def rstd(y):
        return lax.rsqrt(jnp.mean(y * y, axis=0, keepdims=True) + EPS)

    def sqnorm(yb):
        yf = yb.astype(jnp.float32)
        return jnp.sum(yf * yf, axis=0, keepdims=True)

    rep = H_A // G_A
    for h in range(H_A):
        y = p[_R_QA + h * HD_A:_R_QA + (h + 1) * HD_A]
        y2 = p[_R_QA2 + h * HD_A:_R_QA2 + (h + 1) * HD_A]
        q = ((y * tab[_T_QAC:_T_QAS] + y2 * tab[_T_QAS:_T_KAC]) * rstd(y)).astype(bf)
        qat_ref[0, h * DQ_PAD:h * DQ_PAD + HD_A, :] = q
        qat_ref[0, h * DQ_PAD + HD_A:(h + 1) * DQ_PAD, :] = zeros64.astype(bf)
        qan_ref[0, h // rep, h % rep:h % rep + 1, :] = sqnorm(q)
    for g in range(G_A):
        y = p[_R_KA + g * HD_A:_R_KA + (g + 1) * HD_A]
        y2 = p[_R_KA2 + g * HD_A:_R_KA2 + (g + 1) * HD_A]
        k = (y * tab[_T_KAC:_T_KAS] + y2 * tab[_T_KAS:_T_QBC]) * rstd(y)
        ka_ref[0, g] = jnp.concatenate([k, zeros64], axis=0).T.astype(bf)
        kan_ref[0, g, 0:1, :] = sqnorm(k.astype(bf))
        vat_ref[0, g * VROWS_A:g * VROWS_A + HD_A, :] = p[_R_VA + g * HD_A:_R_VA + (g + 1) * HD_A].astype(bf)
        vat_ref[0, g * VROWS_A + HD_A:(g + 1) * VROWS_A, :] = ones

    cq = p[_R_CQ:_R_CKV]
    cqn = (cq * rstd(cq) * gcq_ref[...]).astype(bf)
    q2 = _dot(wuqt_ref[...], cqn)
    for h in range(H_B):
        nope = (q2[h * NOPE_B:(h + 1) * NOPE_B] * sb).astype(bf)
        yr = q2[512 + h * ROPE_B:512 + (h + 1) * ROPE_B]
        yr2 = q2[768 + h * ROPE_B:768 + (h + 1) * ROPE_B]
        rope = (yr * tab[_T_QBC:_T_QBS] + yr2 * tab[_T_QBS:_T_KBC]).astype(bf)
        base = h * DQ_PAD
        qbt_ref[0, base:base + NOPE_B, :] = nope
        qbt_ref[0, base + NOPE_B:base + NOPE_B + ROPE_B, :] = rope
        qbt_ref[0, base + NOPE_B + ROPE_B:base + DQ_PAD, :] = zeros32.astype(bf)
        qbn_ref[0, h // 2, h % 2:h % 2 + 1, :] = sqnorm(nope) + sqnorm(rope)
    ckv = p[_R_CKV:_R_KR]
    ckvn = (ckv * rstd(ckv) * gckv_ref[...]).astype(bf)
    kv = _dot(wukvt_ref[...], ckvn)
    krope = p[_R_KR:_R_KR2] * tab[_T_KBC:_T_KBS] + p[_R_KR2:_R_END] * tab[_T_KBS:_T_END]
    krope_n = sqnorm(krope.astype(bf))
    for h in range(H_B):
        knope = kv[h * NOPE_B:(h + 1) * NOPE_B]
        kt = jnp.concatenate([knope, krope, zeros32], axis=0)
        kb_ref[0, h] = kt.T.astype(bf)
        kbn_ref[0, h // 2, h % 2:h % 2 + 1, :] = sqnorm(knope.astype(bf)) + krope_n
        v0 = H_B * NOPE_B + h * V_B
        vbt_ref[0, h * VROWS_B:h * VROWS_B + V_B, :] = kv[v0:v0 + V_B].astype(bf)
        vbt_ref[0, h * VROWS_B + V_B:(h + 1) * VROWS_B, :] = ones


def _proj(x, W):
    B, S, _ = x.shape
    tm = TM_PROJ
    bf = MM_DTYPE
    f32 = jnp.float32
    rep = H_A // G_A
    const = lambda shape: pl.BlockSpec(shape, lambda b, i: (0,) * len(shape))
    return pl.pallas_call(
        functools.partial(_proj_kernel, sb=W["sb"]),
        grid=(B, S // tm),
        in_specs=[
            pl.BlockSpec((1, tm, D_MODEL), lambda b, i: (b, i, 0)),
            const((1, D_MODEL)), const((_R_END, D_MODEL)),
            pl.BlockSpec((_T_END, tm), lambda b, i: (0, i)),
            const((Q_LORA, tm)), const((KV_LORA, tm)),
            const((1024, Q_LORA)), const((1024, KV_LORA)),
        ],
        out_specs=[
            pl.BlockSpec((1, H_A * DQ_PAD, tm), lambda b, i: (b, 0, i)),
            pl.BlockSpec((1, G_A, tm, DQ_PAD), lambda b, i: (b, 0, i, 0)),
            pl.BlockSpec((1, G_A * VROWS_A, tm), lambda b, i: (b, 0, i)),
            pl.BlockSpec((1, G_A, rep, tm), lambda b, i: (b, 0, 0, i)),
            pl.BlockSpec((1, G_A, 1, tm), lambda b, i: (b, 0, 0, i)),
            pl.BlockSpec((1, H_B * DQ_PAD, tm), lambda b, i: (b, 0, i)),
            pl.BlockSpec((1, H_B, tm, DQ_PAD), lambda b, i: (b, 0, i, 0)),
            pl.BlockSpec((1, H_B * VROWS_B, tm), lambda b, i: (b, 0, i)),
            pl.BlockSpec((1, H_B // 2, 2, tm), lambda b, i: (b, 0, 0, i)),
            pl.BlockSpec((1, H_B // 2, 2, tm), lambda b, i: (b, 0, 0, i)),
        ],
        out_shape=[
            jax.ShapeDtypeStruct((B, H_A * DQ_PAD, S), bf),
            jax.ShapeDtypeStruct((B, G_A, S, DQ_PAD), bf),
            jax.ShapeDtypeStruct((B, G_A * VROWS_A, S), bf),
            jax.ShapeDtypeStruct((B, G_A, rep, S), f32),
            jax.ShapeDtypeStruct((B, G_A, 1, S), f32),
            jax.ShapeDtypeStruct((B, H_B * DQ_PAD, S), bf),
            jax.ShapeDtypeStruct((B, H_B, S, DQ_PAD), bf),
            jax.ShapeDtypeStruct((B, H_B * VROWS_B, S), bf),
            jax.ShapeDtypeStruct((B, H_B // 2, 2, S), f32),
            jax.ShapeDtypeStruct((B, H_B // 2, 2, S), f32),
        ],
        compiler_params=_cparams(("parallel", "parallel")),
        name="proj",
    )(x, W["ln1"], W["w1t"], W["tabs"], W["gcq"], W["gckv"], W["wuqt"], W["wukvt"])


def _attn_kernel(qt_ref, k_ref, vt_ref, qn_ref, kn_ref, ot_ref, s_ref, p_ref, *, heads, rep, dv, tk, heads_per_loop):
    S = k_ref.shape[2]
    tq = qt_ref.shape[2]
    nk = S // tk
    vrows = dv + ONES_ROWS

    def scores(h, c):
        off = pl.multiple_of(c * tk, tk)
        kc = k_ref[0, h // rep, pl.ds(off, tk), :]
        return _dot(kc, qt_ref[0, h * DQ_PAD:(h + 1) * DQ_PAD, :])

    def vchunk(h, c):
        off = pl.multiple_of(c * tk, tk)
        g = h // rep
        return vt_ref[0, g * vrows:(g + 1) * vrows, pl.ds(off, tk)]

    def finish(h, acc):
        ot_ref[0, h * dv:(h + 1) * dv, :] = (acc[:dv] / acc[dv:dv + 1]).astype(ot_ref.dtype)

    zero_acc = lambda: jnp.zeros((vrows, tq), jnp.float32)
    bound2 = jnp.max(qn_ref[0, 0]) * jnp.max(kn_ref[0, 0])
    safe = bound2 <= SAFE_SCORE_BOUND * SAFE_SCORE_BOUND

    def probs(s, m, with_max):
        if not with_max:
            return m, None, jnp.exp2(s).astype(MM_DTYPE)
        mn = jnp.maximum(m, jnp.max(s, axis=0, keepdims=True))
        return mn, jnp.exp2(m - mn), jnp.exp2(s - mn).astype(MM_DTYPE)

    def accumulate(h, c, p_slot, alpha, acc):
        pv = _dot(vchunk(h, c), p_slot[...])
        return pv + (acc if alpha is None else alpha * acc)

    def run(with_max):
        for h0 in range(0, heads, heads_per_loop):
            hs = list(range(h0, h0 + heads_per_loop))
            for i, h in enumerate(hs):
                s_ref[0, i] = scores(h, 0)
                p_ref[1, i] = jnp.zeros((tk, tq), MM_DTYPE)

            def body(j, carry, hs=hs):
                c0 = 2 * j
                prev = jnp.maximum(c0 - 1, 0)
                nxt = jnp.minimum(c0 + 2, nk - 1)
                out = []
                for i, (h, (m, alpha_b, acc)) in enumerate(zip(hs, carry)):
                    s_ref[1, i] = scores(h, c0 + 1)
                    acc = accumulate(h, prev, p_ref.at[1, i], alpha_b, acc)
                    m, alpha_a, pa = probs(s_ref[0, i], m, with_max)
                    p_ref[0, i] = pa
                    acc = accumulate(h, c0, p_ref.at[0, i], alpha_a, acc)
                    s_ref[0, i] = scores(h, nxt)
                    m, alpha_b, pb = probs(s_ref[1, i], m, with_max)
                    p_ref[1, i] = pb
                    out.append((m, alpha_b, acc))
                return tuple(out)

            one = jnp.ones((1, tq), jnp.float32) if with_max else None
            init = tuple((jnp.full((1, tq), -jnp.inf, jnp.float32), one, zero_acc()) for _ in hs)
            res = lax.fori_loop(0, nk // 2, body, init, unroll=ATTN_UNROLL)
            for i, ((m, alpha_b, acc), h) in enumerate(zip(res, hs)):
                finish(h, accumulate(h, nk - 1, p_ref.at[1, i], alpha_b, acc))

    pl.when(safe)(lambda: run(False))
    pl.when(jnp.logical_not(safe))(lambda: run(True))


def _attention(qt, k, vt, qn, kn, *, groups_per_step, rep, dv, name):
    B, _, S = qt.shape
    G = k.shape[1]
    gs = groups_per_step
    heads = gs * rep
    tq = TQ_ATTN
    hpl = min(2, heads)
    vrows = dv + ONES_ROWS
    assert (S // TK_ATTN) % 2 == 0
    kern = functools.partial(_attn_kernel, heads=heads, rep=rep, dv=dv, tk=TK_ATTN, heads_per_loop=hpl)
    return pl.pallas_call(
        kern,
        grid=(B, G // gs, S // tq),
        in_specs=[
            pl.BlockSpec((1, heads * DQ_PAD, tq), lambda b, g, i: (b, g, i)),
            pl.BlockSpec((1, gs, S, DQ_PAD), lambda b, g, i: (b, g, 0, 0)),
            pl.BlockSpec((1, gs * vrows, S), lambda b, g, i: (b, g, 0)),
            pl.BlockSpec((1, 1, heads, tq), lambda b, g, i: (b, g, 0, i)),
            pl.BlockSpec((1, 1, gs, S), lambda b, g, i: (b, g, 0, 0)),
        ],
        out_specs=pl.BlockSpec((1, heads * dv, tq), lambda b, g, i: (b, g, i)),
        out_shape=jax.ShapeDtypeStruct((B, G * rep * dv, S), MM_DTYPE),
        scratch_shapes=[pltpu.VMEM((2, hpl, TK_ATTN, tq), jnp.float32),
                        pltpu.VMEM((2, hpl, TK_ATTN, tq), MM_DTYPE)],
        compiler_params=_cparams(("parallel", "parallel", "arbitrary")),
        name=name,
    )(qt, k, vt, qn, kn)


def _post_kernel(x_ref, ln1_ref, wgt_ref, oat_ref, obt_ref, wbrat_ref, wbrbt_ref, woutt_ref, g2_ref,
                 ht_ref, xn2t_ref):
    bf = MM_DTYPE
    x = x_ref[0]
    xn = x * lax.rsqrt(jnp.mean(x * x, axis=-1, keepdims=True) + EPS) * ln1_ref[...]
    xnt = xn.T.astype(bf)
    gates = jax.nn.sigmoid(_dot(wgt_ref[...], xnt))
    ma = _dot(wbrat_ref[...], oat_ref[0])
    mb = _dot(wbrbt_ref[...], obt_ref[0])
    merged = gates[:D_MODEL] * ma + gates[D_MODEL:] * mb
    ht = x.T + _dot(woutt_ref[...], merged.astype(bf))
    ht_ref[0] = ht
    r2 = lax.rsqrt(jnp.mean(ht * ht, axis=0, keepdims=True) + EPS)
    xn2t_ref[0] = (ht * r2 * g2_ref[...]).astype(bf)


def _post(x, oat, obt, W):
    B, S, _ = x.shape
    tm = TM_POST
    const = lambda shape: pl.BlockSpec(shape, lambda b, i: (0,) * len(shape))
    return pl.pallas_call(
        _post_kernel,
        grid=(B, S // tm),
        in_specs=[
            pl.BlockSpec((1, tm, D_MODEL), lambda b, i: (b, i, 0)),
            const((1, D_MODEL)), const((2 * D_MODEL, D_MODEL)),
            pl.BlockSpec((1, H_A * HD_A, tm), lambda b, i: (b, 0, i)),
            pl.BlockSpec((1, H_B * V_B, tm), lambda b, i: (b, 0, i)),
            const((D_MODEL, H_A * HD_A)), const((D_MODEL, H_B * V_B)), const((D_MODEL, D_MODEL)),
            const((D_MODEL, tm)),
        ],
        out_specs=[
            pl.BlockSpec((1, D_MODEL, tm), lambda b, i: (b, 0, i)),
            pl.BlockSpec((1, D_MODEL, tm), lambda b, i: (b, 0, i)),
        ],
        out_shape=[
            jax.ShapeDtypeStruct((B, D_MODEL, S), jnp.float32),
            jax.ShapeDtypeStruct((B, D_MODEL, S), MM_DTYPE),
        ],
        compiler_params=_cparams(("parallel", "parallel")),
        name="post",
    )(x, W["ln1"], W["wgt"], oat, obt, W["wbrat"], W["wbrbt"], W["woutt"], W["g2"])


def _cmpx(xs, i, j):
    a, b = xs[i], xs[j]
    xs[i], xs[j] = jnp.maximum(a, b), jnp.minimum(a, b)


def _bitonic_merge_desc(xs):
    n = len(xs)
    j = n // 2
    while j >= 1:
        for i in range(n):
            l = i ^ j
            if l > i:
                _cmpx(xs, i, l)
        j //= 2


def _bitonic_sort_desc(xs):
    n = len(xs)
    k = 2
    while k <= n:
        j = k // 2
        while j >= 1:
            for i in range(n):
                l = i ^ j
                if l > i:
                    if (i & k) == 0:
                        _cmpx(xs, i, l)
                    else:
                        _cmpx(xs, l, i)
            j //= 2
        k *= 2


_CAND_PAIRS = [(i, j) for i in range(PEER_TOPK) for j in range(PEER_TOPK) if (i + 1) * (j + 1) <= PEER_TOPK]


def _route_kernel(xn2t_ref, wpqt_ref, subk_ref, s_ref, stats_ref):
    bf = MM_DTYPE
    tr = xn2t_ref.shape[2]
    groups = tr // LANES
    qt = _dot(wpqt_ref[...], xn2t_ref[0]).astype(bf)
    row = lax.broadcasted_iota(jnp.int32, (SUBLANES, LANES), 0)
    nblk = N_KEYS // SUBLANES
    for h in range(PEER_HEADS):
        tops = []
        for pp in range(2):
            hp = h * 2 + pp
            st = _dot(subk_ref[hp], qt[hp * HALF_KEY:(hp + 1) * HALF_KEY])
            s_ref[0, hp * N_KEYS:(hp + 1) * N_KEYS, :] = st
            xs = [st[v * SUBLANES:(v + 1) * SUBLANES] for v in range(nblk)]
            _bitonic_sort_desc(xs)
            for shift in (4, 2, 1):
                rolled = [pltpu.roll(xs[nblk - 1 - v], shift, 0) for v in range(nblk)]
                xs = [jnp.maximum(a, b) for a, b in zip(xs, rolled)]
                _bitonic_merge_desc(xs)
            comp = []
            for v in range(PEER_TOPK):
                c = xs[v][:, 0:LANES]
                for gidx in range(1, groups):
                    c = jnp.where(row == gidx, xs[v][:, gidx * LANES:(gidx + 1) * LANES], c)
                comp.append(c)
            tops.append(comp)
        a, b = tops
        cands = [a[i] + b[j] for (i, j) in _CAND_PAIRS]
        cands += [jnp.full((SUBLANES, LANES), -jnp.inf, jnp.float32)] * (64 - len(cands))
        _bitonic_sort_desc(cands)
        cmax = cands[0]
        z = jnp.zeros((SUBLANES, LANES), jnp.float32)
        for kk in range(PEER_TOPK):
            z = z + jnp.exp(cands[kk] - cmax)
        for r, val in enumerate((cands[PEER_TOPK - 1], a[0], b[0], z)):
            wide = jnp.concatenate([val[gidx:gidx + 1, :] for gidx in range(groups)], axis=1)
            stats_ref[0, h * 4 + r:h * 4 + r + 1, :] = wide


def _route(xn2t, W):
    B, _, S = xn2t.shape
    tr = TR_ROUTE
    const = lambda shape: pl.BlockSpec(shape, lambda b, i: (0,) * len(shape))
    return pl.pallas_call(
        _route_kernel,
        grid=(B, S // tr),
        in_specs=[
            pl.BlockSpec((1, D_MODEL, tr), lambda b, i: (b, 0, i)),
            const((PEER_HEADS * D_KEY, D_MODEL)),
            const((PEER_HEADS * 2, N_KEYS, HALF_KEY)),
        ],
        out_specs=[
            pl.BlockSpec((1, PEER_HEADS * 2 * N_KEYS, tr), lambda b, i: (b, 0, i)),
            pl.BlockSpec((1, PEER_HEADS * 4, tr), lambda b, i: (b, 0, i)),
        ],
        out_shape=[
            jax.ShapeDtypeStruct((B, PEER_HEADS * 2 * N_KEYS, S), jnp.float32),
            jax.ShapeDtypeStruct((B, PEER_HEADS * 4, S), jnp.float32),
        ],
        compiler_params=_cparams(("parallel", "parallel")),
        name="route",
    )(xn2t, W["wpqt"], W["subk"])


def _peer_kernel(xn2t_ref, s_ref, stats_ref, ht_ref, u_ref, vt_ref, gfin_ref, y_ref,
                 e1_ref, e2_ref, acc_ref, hid_ref, a_ref, s1_ref, s2_ref, tau_ref):
    e = pl.program_id(2)
    ne = pl.num_programs(2)
    ec = u_ref.shape[0]
    tt = xn2t_ref.shape[2]
    rows_per_chunk = ec // N_KEYS

    @pl.when(e == 0)
    def _():
        for h in range(PEER_HEADS):
            amax = stats_ref[0, h * 4 + 1:h * 4 + 2, :]
            bmax = stats_ref[0, h * 4 + 2:h * 4 + 3, :]
            z = stats_ref[0, h * 4 + 3:h * 4 + 4, :]
            s1 = s_ref[0, (2 * h) * N_KEYS:(2 * h + 1) * N_KEYS, :]
            s2 = s_ref[0, (2 * h + 1) * N_KEYS:(2 * h + 2) * N_KEYS, :]
            e1_ref[h * N_KEYS:(h + 1) * N_KEYS, :] = jnp.exp(s1 - amax)
            e2_ref[h * N_KEYS:(h + 1) * N_KEYS, :] = jnp.exp(s2 - bmax) / z
            s1_ref[h * N_KEYS:(h + 1) * N_KEYS, :] = s1
            s2_ref[h * N_KEYS:(h + 1) * N_KEYS, :] = s2
            tau_ref[h:h + 1, :] = stats_ref[0, h * 4:h * 4 + 1, :]
        acc_ref[...] = jnp.zeros_like(acc_ref)

    def build(rs):
        first = [e * rows_per_chunk + r for r in rs]
        s1rows = [[s1_ref[pl.ds(h * N_KEYS + i, 1), :] for h in range(PEER_HEADS)] for i in first]
        e1rows = [[e1_ref[pl.ds(h * N_KEYS + i, 1), :] for h in range(PEER_HEADS)] for i in first]
        for t0 in range(0, tt, LANES):
            lanes = slice(t0, t0 + LANES)
            for j0 in range(0, N_KEYS, PEER_TJ):
                ws = [jnp.zeros((PEER_TJ, LANES), jnp.float32) for _ in rs]
                for h in range(PEER_HEADS):
                    tau = tau_ref[h:h + 1, lanes]
                    s2 = s2_ref[h * N_KEYS + j0:h * N_KEYS + j0 + PEER_TJ, lanes]
                    e2 = e2_ref[h * N_KEYS + j0:h * N_KEYS + j0 + PEER_TJ, lanes]
                    for k in range(len(rs)):
                        ws[k] = ws[k] + jnp.where(s1rows[k][h][:, lanes] + s2 >= tau,
                                                  e1rows[k][h][:, lanes] * e2, 0.0)
                for k, r in enumerate(rs):
                    rows = slice(r * N_KEYS + j0, r * N_KEYS + j0 + PEER_TJ)
                    hid = hid_ref[rows, lanes]
                    act = 0.5 * hid * (1.0 + lax.erf(hid * (2.0 ** -0.5)))
                    a_ref[rows, lanes] = (act * ws[k]).astype(MM_DTYPE)

    half = ec // 2
    rows_half = rows_per_chunk // 2
    xt = xn2t_ref[0]
    hid_ref[:half] = _dot(u_ref[:half], xt)
    hid_ref[half:] = _dot(u_ref[half:], xt)
    build(list(range(rows_half)))
    acc_ref[...] += _dot(vt_ref[:, :half], a_ref[:half])
    build(list(range(rows_half, rows_per_chunk)))
    acc_ref[...] += _dot(vt_ref[:, half:], a_ref[half:])

    @pl.when(e == ne - 1)
    def _():
        yt = ht_ref[0] + acc_ref[...]
        r = lax.rsqrt(jnp.mean(yt * yt, axis=0, keepdims=True) + EPS)
        y_ref[0] = (yt * r * gfin_ref[...]).T


def _peer(xn2t, s, stats, ht, W):
    B, _, S = xn2t.shape
    tt, ec = TT_PEER, EC_PEER
    return pl.pallas_call(
        _peer_kernel,
        grid=(B, S // tt, N_EXPERTS // ec),
        in_specs=[
            pl.BlockSpec((1, D_MODEL, tt), lambda b, i, e: (b, 0, i)),
            pl.BlockSpec((1, PEER_HEADS * 2 * N_KEYS, tt), lambda b, i, e: (b, 0, i)),
            pl.BlockSpec((1, PEER_HEADS * 4, tt), lambda b, i, e: (b, 0, i)),
            pl.BlockSpec((1, D_MODEL, tt), lambda b, i, e: (b, 0, i)),
            pl.BlockSpec((ec, D_MODEL), lambda b, i, e: (e, 0)),
            pl.BlockSpec((D_MODEL, ec), lambda b, i, e: (0, e)),
            pl.BlockSpec((D_MODEL, tt), lambda b, i, e: (0, 0)),
        ],
        out_specs=pl.BlockSpec((1, tt, D_MODEL), lambda b, i, e: (b, i, 0)),
        out_shape=jax.ShapeDtypeStruct((B, S, D_MODEL), jnp.float32),
        scratch_shapes=[
            pltpu.VMEM((PEER_HEADS * N_KEYS, tt), jnp.float32),
            pltpu.VMEM((PEER_HEADS * N_KEYS, tt), jnp.float32),
            pltpu.VMEM((D_MODEL, tt), jnp.float32),
            pltpu.VMEM((ec, tt), jnp.float32),
            pltpu.VMEM((ec, tt), MM_DTYPE),
            pltpu.VMEM((PEER_HEADS * N_KEYS, tt), jnp.float32),
            pltpu.VMEM((PEER_HEADS * N_KEYS, tt), jnp.float32),
            pltpu.VMEM((PEER_HEADS, tt), jnp.float32),
        ],
        compiler_params=_cparams(("parallel", "parallel", "arbitrary")),
        name="peer",
    )(xn2t, s, stats, ht, W["u"], W["vt"], W["gfin"])


def _forward(x, W):
    qat, ka, vat, qan, kan, qbt, kb, vbt, qbn, kbn = _proj(x, W)
    oat = _attention(qat, ka, vat, qan, kan, groups_per_step=1, rep=H_A // G_A, dv=HD_A, name="attn_gqa")
    obt = _attention(qbt, kb, vbt, qbn, kbn, groups_per_step=2, rep=1, dv=V_B, name="attn_mla")
    ht, xn2t = _post(x, oat, obt, W)
    s, stats = _route(xn2t, W)
    return _peer(xn2t, s, stats, ht, W)


def kernel(x_prompt, x_sample, ln1_g, w_in, qa_norm_g, ka_norm_g, cq_norm_g, ckv_norm_g, w_uq, w_ukv,
           w_br_a, w_br_b, w_out, ln2_g, w_pq, sub_keys, expert_u, expert_v, final_g):
    params = (ln1_g, w_in, qa_norm_g, ka_norm_g, cq_norm_g, ckv_norm_g, w_uq, w_ukv,
              w_br_a, w_br_b, w_out, ln2_g, w_pq, sub_keys, expert_u, expert_v, final_g)
    assert x_prompt.shape[1] == x_sample.shape[1]
    W = _prepare(x_prompt.shape[1], *params)
    return (_forward(x_prompt, W), _forward(x_sample, W))
```

```python
import functools
import math

import numpy as np
import jax
import jax.numpy as jnp
from jax import lax
from jax.experimental import pallas as pl
from jax.experimental.pallas import tpu as pltpu

D_MODEL = 1024
GRID_W = 64
ROPE_THETA = 10000.0
EPS = 1e-6
H_A, G_A, HD_A = 8, 2, 64
H_B, Q_LORA, KV_LORA, NOPE_B, ROPE_B, V_B = 8, 256, 128, 64, 32, 64
PEER_HEADS, N_KEYS, D_KEY, PEER_TOPK = 8, 128, 256, 16
N_EXPERTS = N_KEYS * N_KEYS
HALF_KEY = D_KEY // 2

LANES = 128
SUBLANES = 8
VMEM_LIMIT_BYTES = 56 * 1024 * 1024

DQ_PAD = LANES
LOG2E = math.log2(math.e)
MM_DTYPE = jnp.bfloat16
ONES_ROWS = 2 * SUBLANES
VROWS_A = HD_A + ONES_ROWS
VROWS_B = V_B + ONES_ROWS
SAFE_SCORE_BOUND = 60.0

TM_PROJ = 512
TQ_ATTN = 512
TK_ATTN = 256
ATTN_UNROLL = 16
TM_POST = 512
TR_ROUTE = SUBLANES * LANES
TT_PEER = 512
EC_PEER = 1024
PEER_TJ = 32

_R_QA, _R_QA2, _R_KA, _R_KA2, _R_VA, _R_CQ, _R_CKV, _R_KR, _R_KR2, _R_END = (
    0, 512, 1024, 1152, 1280, 1408, 1664, 1792, 1824, 1856)
_T_QAC, _T_QAS, _T_KAC, _T_KAS, _T_QBC, _T_QBS, _T_KBC, _T_KBS, _T_END = (
    0, 64, 128, 192, 256, 288, 320, 352, 384)


def _cparams(sem, flags=None):
    return pltpu.CompilerParams(dimension_semantics=sem, vmem_limit_bytes=VMEM_LIMIT_BYTES, flags=flags)


def _dot(a, b):
    return jnp.dot(a, b, preferred_element_type=jnp.float32)


def _rope_partner(d):
    half, qtr = d // 2, d // 4
    e = np.arange(d)
    within = e % half
    first = within < qtr
    partner = np.where(first, e + qtr, e - qtr)
    sign = np.where(first, -1.0, 1.0).astype(np.float32)
    freq_idx = within % qtr
    use_col = e >= half
    return partner, sign, freq_idx, use_col


def _rope_tables(S, d):
    half = d // 2
    partner, sign, freq_idx, use_col = _rope_partner(d)
    t = jnp.arange(S, dtype=jnp.int32)
    r = (t // GRID_W).astype(jnp.float32)
    c = (t % GRID_W).astype(jnp.float32)
    freqs = ROPE_THETA ** (-jnp.arange(0, half, 2, dtype=jnp.float32) / half)
    f = freqs[freq_idx]
    pos = jnp.where(jnp.asarray(use_col)[:, None], c[None, :], r[None, :])
    ang = pos * f[:, None]
    return jnp.cos(ang), jnp.sin(ang) * jnp.asarray(sign)[:, None], partner


def _prepare(S, ln1_g, w_in, qa_norm_g, ka_norm_g, cq_norm_g, ckv_norm_g, w_uq, w_ukv,
             w_br_a, w_br_b, w_out, ln2_g, w_pq, sub_keys, expert_u, expert_v, final_g):
    bf = MM_DTYPE
    w = w_in[0]
    c_qa, c_ka, c_va, c_cq, c_ckv, c_kr, c_ga = np.cumsum([512, 128, 128, 256, 128, 32, 1024]).tolist()
    p64, _, _, _ = _rope_partner(HD_A)
    p32, _, _, _ = _rope_partner(ROPE_B)
    qa_perm = (np.arange(H_A)[:, None] * HD_A + p64[None, :]).reshape(-1)
    ka_perm = (np.arange(G_A)[:, None] * HD_A + p64[None, :]).reshape(-1)
    w_qa, w_ka = w[:, :c_qa], w[:, c_qa:c_ka]
    w_kr = w[:, c_ckv:c_kr]
    w1 = jnp.concatenate([
        w_qa, w_qa[:, qa_perm], w_ka, w_ka[:, ka_perm], w[:, c_ka:c_va],
        w[:, c_va:c_cq], w[:, c_cq:c_ckv], w_kr, w_kr[:, p32]], axis=1)
    w1t = w1.T.astype(bf)
    wgt = w[:, c_kr:].T.astype(bf)

    uq = w_uq[0].reshape(Q_LORA, H_B, NOPE_B + ROPE_B)
    uq_nope = uq[:, :, :NOPE_B].reshape(Q_LORA, H_B * NOPE_B)
    uq_rope = uq[:, :, NOPE_B:]
    wuqt = jnp.concatenate([uq_nope, uq_rope.reshape(Q_LORA, -1),
                            uq_rope[:, :, p32].reshape(Q_LORA, -1)], axis=1).T.astype(bf)
    ukv = w_ukv[0].reshape(KV_LORA, H_B, NOPE_B + V_B)
    wukvt = jnp.concatenate([ukv[:, :, :NOPE_B].reshape(KV_LORA, -1),
                             ukv[:, :, NOPE_B:].reshape(KV_LORA, -1)], axis=1).T.astype(bf)

    cos64, sin64, _ = _rope_tables(S, HD_A)
    cos32, sin32, _ = _rope_tables(S, ROPE_B)
    sa = (HD_A ** -0.5) * LOG2E
    sb = ((NOPE_B + ROPE_B) ** -0.5) * LOG2E
    gq, gk = qa_norm_g[0], ka_norm_g[0]
    tabs = jnp.concatenate([
        cos64 * (gq * sa)[:, None], sin64 * (gq[p64] * sa)[:, None],
        cos64 * gk[:, None], sin64 * gk[p64][:, None],
        cos32 * sb, sin32 * sb, cos32, sin32], axis=0)

    def bcast(g, n):
        return jnp.broadcast_to(g.reshape(-1, 1), (g.size, n))

    return dict(
        ln1=ln1_g[0].reshape(1, D_MODEL), w1t=w1t, wgt=wgt, wuqt=wuqt, wukvt=wukvt, tabs=tabs,
        gcq=bcast(cq_norm_g[0], TM_PROJ), gckv=bcast(ckv_norm_g[0], TM_PROJ),
        wbrat=w_br_a[0].T.astype(bf), wbrbt=w_br_b[0].T.astype(bf), woutt=w_out[0].T.astype(bf),
        g2=bcast(ln2_g[0], TM_POST), wpqt=w_pq[0].T.astype(bf),
        subk=sub_keys[0].reshape(PEER_HEADS * 2, N_KEYS, HALF_KEY).astype(bf),
        u=expert_u[0].astype(bf), vt=expert_v[0].T.astype(bf),
        gfin=bcast(final_g, TT_PEER), sb=sb)


def _proj_kernel(x_ref, ln1_ref, w1t_ref, tab_ref, gcq_ref, gckv_ref, wuqt_ref, wukvt_ref,
                 qat_ref, ka_ref, vat_ref, qan_ref, kan_ref, qbt_ref, kb_ref, vbt_ref, qbn_ref, kbn_ref, *, sb):
    bf = MM_DTYPE
    x = x_ref[0]
    tm = x.shape[0]
    xn = x * lax.rsqrt(jnp.mean(x * x, axis=-1, keepdims=True) + EPS) * ln1_ref[...]
    xnt = xn.T.astype(bf)
    p = _dot(w1t_ref[...], xnt)
    tab = tab_ref[...]
    zeros64 = jnp.zeros((64, tm), jnp.float32)
    zeros32 = jnp.zeros((32, tm), jnp.float32)
    ones = jnp.ones((ONES_ROWS, tm), bf)

    def rstd(y):
        return lax.rsqrt(jnp.mean(y * y, axis=0, keepdims=True) + EPS)

    def sqnorm(yb):
        yf = yb.astype(jnp.float32)
        return jnp.sum(yf * yf, axis=0, keepdims=True)

    rep = H_A // G_A
    for h in range(H_A):
        y = p[_R_QA + h * HD_A:_R_QA + (h + 1) * HD_A]
        y2 = p[_R_QA2 + h * HD_A:_R_QA2 + (h + 1) * HD_A]
        q = ((y * tab[_T_QAC:_T_QAS] + y2 * tab[_T_QAS:_T_KAC]) * rstd(y)).astype(bf)
        qat_ref[0, h * DQ_PAD:h * DQ_PAD + HD_A, :] = q
        qat_ref[0, h * DQ_PAD + HD_A:(h + 1) * DQ_PAD, :] = zeros64.astype(bf)
        qan_ref[0, h // rep, h % rep:h % rep + 1, :] = sqnorm(q)
    for g in range(G_A):
        y = p[_R_KA + g * HD_A:_R_KA + (g + 1) * HD_A]
        y2 = p[_R_KA2 + g * HD_A:_R_KA2 + (g + 1) * HD_A]
        k = (y * tab[_T_KAC:_T_KAS] + y2 * tab[_T_KAS:_T_QBC]) * rstd(y)
        ka_ref[0, g] = jnp.concatenate([k, zeros64], axis=0).T.astype(bf)
        kan_ref[0, g, 0:1, :] = sqnorm(k.astype(bf))
        vat_ref[0, g * VROWS_A:g * VROWS_A + HD_A, :] = p[_R_VA + g * HD_A:_R_VA + (g + 1) * HD_A].astype(bf)
        vat_ref[0, g * VROWS_A + HD_A:(g + 1) * VROWS_A, :] = ones

    cq = p[_R_CQ:_R_CKV]
    cqn = (cq * rstd(cq) * gcq_ref[...]).astype(bf)
    q2 = _dot(wuqt_ref[...], cqn)
    for h in range(H_B):
        nope = (q2[h * NOPE_B:(h + 1) * NOPE_B] * sb).astype(bf)
        yr = q2[512 + h * ROPE_B:512 + (h + 1) * ROPE_B]
        yr2 = q2[768 + h * ROPE_B:768 + (h + 1) * ROPE_B]
        rope = (yr * tab[_T_QBC:_T_QBS] + yr2 * tab[_T_QBS:_T_KBC]).astype(bf)
        base = h * DQ_PAD
        qbt_ref[0, base:base + NOPE_B, :] = nope
        qbt_ref[0, base + NOPE_B:base + NOPE_B + ROPE_B, :] = rope
        qbt_ref[0, base + NOPE_B + ROPE_B:base + DQ_PAD, :] = zeros32.astype(bf)
        qbn_ref[0, h // 2, h % 2:h % 2 + 1, :] = sqnorm(nope) + sqnorm(rope)
    ckv = p[_R_CKV:_R_KR]
    ckvn = (ckv * rstd(ckv) * gckv_ref[...]).astype(bf)
    kv = _dot(wukvt_ref[...], ckvn)
    krope = p[_R_KR:_R_KR2] * tab[_T_KBC:_T_KBS] + p[_R_KR2:_R_END] * tab[_T_KBS:_T_END]
    krope_n = sqnorm(krope.astype(bf))
    for h in range(H_B):
        knope = kv[h * NOPE_B:(h + 1) * NOPE_B]
        kt = jnp.concatenate([knope, krope, zeros32], axis=0)
        kb_ref[0, h] = kt.T.astype(bf)
        kbn_ref[0, h // 2, h % 2:h % 2 + 1, :] = sqnorm(knope.astype(bf)) + krope_n
        v0 = H_B * NOPE_B + h * V_B
        vbt_ref[0, h * VROWS_B:h * VROWS_B + V_B, :] = kv[v0:v0 + V_B].astype(bf)
        vbt_ref[0, h * VROWS_B + V_B:(h + 1) * VROWS_B, :] = ones


def _proj(x, W):
    B, S, _ = x.shape
    tm = TM_PROJ
    bf = MM_DTYPE
    f32 = jnp.float32
    rep = H_A // G_A
    const = lambda shape: pl.BlockSpec(shape, lambda b, i: (0,) * len(shape))
    return pl.pallas_call(
        functools.partial(_proj_kernel, sb=W["sb"]),
        grid=(B, S // tm),
        in_specs=[
            pl.BlockSpec((1, tm, D_MODEL), lambda b, i: (b, i, 0)),
            const((1, D_MODEL)), const((_R_END, D_MODEL)),
            pl.BlockSpec((_T_END, tm), lambda b, i: (0, i)),
            const((Q_LORA, tm)), const((KV_LORA, tm)),
            const((1024, Q_LORA)), const((1024, KV_LORA)),
        ],
        out_specs=[
            pl.BlockSpec((1, H_A * DQ_PAD, tm), lambda b, i: (b, 0, i)),
            pl.BlockSpec((1, G_A, tm, DQ_PAD), lambda b, i: (b, 0, i, 0)),
            pl.BlockSpec((1, G_A * VROWS_A, tm), lambda b, i: (b, 0, i)),
            pl.BlockSpec((1, G_A, rep, tm), lambda b, i: (b, 0, 0, i)),
            pl.BlockSpec((1, G_A, 1, tm), lambda b, i: (b, 0, 0, i)),
            pl.BlockSpec((1, H_B * DQ_PAD, tm), lambda b, i: (b, 0, i)),
            pl.BlockSpec((1, H_B, tm, DQ_PAD), lambda b, i: (b, 0, i, 0)),
            pl.BlockSpec((1, H_B * VROWS_B, tm), lambda b, i: (b, 0, i)),
            pl.BlockSpec((1, H_B // 2, 2, tm), lambda b, i: (b, 0, 0, i)),
            pl.BlockSpec((1, H_B // 2, 2, tm), lambda b, i: (b, 0, 0, i)),
        ],
        out_shape=[
            jax.ShapeDtypeStruct((B, H_A * DQ_PAD, S), bf),
            jax.ShapeDtypeStruct((B, G_A, S, DQ_PAD), bf),
            jax.ShapeDtypeStruct((B, G_A * VROWS_A, S), bf),
            jax.ShapeDtypeStruct((B, G_A, rep, S), f32),
            jax.ShapeDtypeStruct((B, G_A, 1, S), f32),
            jax.ShapeDtypeStruct((B, H_B * DQ_PAD, S), bf),
            jax.ShapeDtypeStruct((B, H_B, S, DQ_PAD), bf),
            jax.ShapeDtypeStruct((B, H_B * VROWS_B, S), bf),
            jax.ShapeDtypeStruct((B, H_B // 2, 2, S), f32),
            jax.ShapeDtypeStruct((B, H_B // 2, 2, S), f32),
        ],
        compiler_params=_cparams(("parallel", "parallel")),
        name="proj",
    )(x, W["ln1"], W["w1t"], W["tabs"], W["gcq"], W["gckv"], W["wuqt"], W["wukvt"])


def _attn_kernel(qt_ref, k_ref, vt_ref, qn_ref, kn_ref, ot_ref, s_ref, p_ref, *, heads, rep, dv, tk, heads_per_loop):
    S = k_ref.shape[2]
    tq = qt_ref.shape[2]
    nk = S // tk
    vrows = dv + ONES_ROWS

    def scores(h, c):
        off = pl.multiple_of(c * tk, tk)
        kc = k_ref[0, h // rep, pl.ds(off, tk), :]
        return _dot(kc, qt_ref[0, h * DQ_PAD:(h + 1) * DQ_PAD, :])

    def vchunk(h, c):
        off = pl.multiple_of(c * tk, tk)
        g = h // rep
        return vt_ref[0, g * vrows:(g + 1) * vrows, pl.ds(off, tk)]

    def finish(h, acc):
        ot_ref[0, h * dv:(h + 1) * dv, :] = (acc[:dv] / acc[dv:dv + 1]).astype(ot_ref.dtype)

    zero_acc = lambda: jnp.zeros((vrows, tq), jnp.float32)
    bound2 = jnp.max(qn_ref[0, 0]) * jnp.max(kn_ref[0, 0])
    safe = bound2 <= SAFE_SCORE_BOUND * SAFE_SCORE_BOUND

    def probs(s, m, with_max):
        if not with_max:
            return m, None, jnp.exp2(s).astype(MM_DTYPE)
        mn = jnp.maximum(m, jnp.max(s, axis=0, keepdims=True))
        return mn, jnp.exp2(m - mn), jnp.exp2(s - mn).astype(MM_DTYPE)

    def accumulate(h, c, p_slot, alpha, acc):
        pv = _dot(vchunk(h, c), p_slot[...])
        return pv + (acc if alpha is None else alpha * acc)

    def run(with_max):
        for h0 in range(0, heads, heads_per_loop):
            hs = list(range(h0, h0 + heads_per_loop))
            for i, h in enumerate(hs):
                s_ref[0, i] = scores(h, 0)
                p_ref[1, i] = jnp.zeros((tk, tq), MM_DTYPE)

            def body(j, carry, hs=hs):
                c0 = 2 * j
                prev = jnp.maximum(c0 - 1, 0)
                nxt = jnp.minimum(c0 + 2, nk - 1)
                out = []
                for i, (h, (m, alpha_b, acc)) in enumerate(zip(hs, carry)):
                    s_ref[1, i] = scores(h, c0 + 1)
                    acc = accumulate(h, prev, p_ref.at[1, i], alpha_b, acc)
                    m, alpha_a, pa = probs(s_ref[0, i], m, with_max)
                    p_ref[0, i] = pa
                    acc = accumulate(h, c0, p_ref.at[0, i], alpha_a, acc)
                    s_ref[0, i] = scores(h, nxt)
                    m, alpha_b, pb = probs(s_ref[1, i], m, with_max)
                    p_ref[1, i] = pb
                    out.append((m, alpha_b, acc))
                return tuple(out)

            one = jnp.ones((1, tq), jnp.float32) if with_max else None
            init = tuple((jnp.full((1, tq), -jnp.inf, jnp.float32), one, zero_acc()) for _ in hs)
            res = lax.fori_loop(0, nk // 2, body, init, unroll=ATTN_UNROLL)
            for i, ((m, alpha_b, acc), h) in enumerate(zip(res, hs)):
                finish(h, accumulate(h, nk - 1, p_ref.at[1, i], alpha_b, acc))

    pl.when(safe)(lambda: run(False))
    pl.when(jnp.logical_not(safe))(lambda: run(True))


def _attention(qt, k, vt, qn, kn, *, groups_per_step, rep, dv, name):
    B, _, S = qt.shape
    G = k.shape[1]
    gs = groups_per_step
    heads = gs * rep
    tq = TQ_ATTN
    hpl = min(2, heads)
    vrows = dv + ONES_ROWS
    assert (S // TK_ATTN) % 2 == 0
    kern = functools.partial(_attn_kernel, heads=heads, rep=rep, dv=dv, tk=TK_ATTN, heads_per_loop=hpl)
    return pl.pallas_call(
        kern,
        grid=(B, G // gs, S // tq),
        in_specs=[
            pl.BlockSpec((1, heads * DQ_PAD, tq), lambda b, g, i: (b, g, i)),
            pl.BlockSpec((1, gs, S, DQ_PAD), lambda b, g, i: (b, g, 0, 0)),
            pl.BlockSpec((1, gs * vrows, S), lambda b, g, i: (b, g, 0)),
            pl.BlockSpec((1, 1, heads, tq), lambda b, g, i: (b, g, 0, i)),
            pl.BlockSpec((1, 1, gs, S), lambda b, g, i: (b, g, 0, 0)),
        ],
        out_specs=pl.BlockSpec((1, heads * dv, tq), lambda b, g, i: (b, g, i)),
        out_shape=jax.ShapeDtypeStruct((B, G * rep * dv, S), MM_DTYPE),
        scratch_shapes=[pltpu.VMEM((2, hpl, TK_ATTN, tq), jnp.float32),
                        pltpu.VMEM((2, hpl, TK_ATTN, tq), MM_DTYPE)],
        compiler_params=_cparams(("parallel", "parallel", "arbitrary")),
        name=name,
    )(qt, k, vt, qn, kn)


def _post_kernel(x_ref, ln1_ref, wgt_ref, oat_ref, obt_ref, wbrat_ref, wbrbt_ref, woutt_ref, g2_ref,
                 ht_ref, xn2t_ref):
    bf = MM_DTYPE
    x = x_ref[0]
    xn = x * lax.rsqrt(jnp.mean(x * x, axis=-1, keepdims=True) + EPS) * ln1_ref[...]
    xnt = xn.T.astype(bf)
    gates = jax.nn.sigmoid(_dot(wgt_ref[...], xnt))
    ma = _dot(wbrat_ref[...], oat_ref[0])
    mb = _dot(wbrbt_ref[...], obt_ref[0])
    merged = gates[:D_MODEL] * ma + gates[D_MODEL:] * mb
    ht = x.T + _dot(woutt_ref[...], merged.astype(bf))
    ht_ref[0] = ht
    r2 = lax.rsqrt(jnp.mean(ht * ht, axis=0, keepdims=True) + EPS)
    xn2t_ref[0] = (ht * r2 * g2_ref[...]).astype(bf)


def _post(x, oat, obt, W):
    B, S, _ = x.shape
    tm = TM_POST
    const = lambda shape: pl.BlockSpec(shape, lambda b, i: (0,) * len(shape))
    return pl.pallas_call(
        _post_kernel,
        grid=(B, S // tm),
        in_specs=[
            pl.BlockSpec((1, tm, D_MODEL), lambda b, i: (b, i, 0)),
            const((1, D_MODEL)), const((2 * D_MODEL, D_MODEL)),
            pl.BlockSpec((1, H_A * HD_A, tm), lambda b, i: (b, 0, i)),
            pl.BlockSpec((1, H_B * V_B, tm), lambda b, i: (b, 0, i)),
            const((D_MODEL, H_A * HD_A)), const((D_MODEL, H_B * V_B)), const((D_MODEL, D_MODEL)),
            const((D_MODEL, tm)),
        ],
        out_specs=[
            pl.BlockSpec((1, D_MODEL, tm), lambda b, i: (b, 0, i)),
            pl.BlockSpec((1, D_MODEL, tm), lambda b, i: (b, 0, i)),
        ],
        out_shape=[
            jax.ShapeDtypeStruct((B, D_MODEL, S), jnp.float32),
            jax.ShapeDtypeStruct((B, D_MODEL, S), MM_DTYPE),
        ],
        compiler_params=_cparams(("parallel", "parallel")),
        name="post",
    )(x, W["ln1"], W["wgt"], oat, obt, W["wbrat"], W["wbrbt"], W["woutt"], W["g2"])


def _cmpx(xs, i, j):
    a, b = xs[i], xs[j]
    xs[i], xs[j] = jnp.maximum(a, b), jnp.minimum(a, b)


def _bitonic_merge_desc(xs):
    n = len(xs)
    j = n // 2
    while j >= 1:
        for i in range(n):
            l = i ^ j
            if l > i:
                _cmpx(xs, i, l)
        j //= 2


def _bitonic_sort_desc(xs):
    n = len(xs)
    k = 2
    while k <= n:
        j = k // 2
        while j >= 1:
            for i in range(n):
                l = i ^ j
                if l > i:
                    if (i & k) == 0:
                        _cmpx(xs, i, l)
                    else:
                        _cmpx(xs, l, i)
            j //= 2
        k *= 2


_CAND_PAIRS = [(i, j) for i in range(PEER_TOPK) for j in range(PEER_TOPK) if (i + 1) * (j + 1) <= PEER_TOPK]


def _route_kernel(xn2t_ref, wpqt_ref, subk_ref, s_ref, stats_ref):
    bf = MM_DTYPE
    tr = xn2t_ref.shape[2]
    groups = tr // LANES
    qt = _dot(wpqt_ref[...], xn2t_ref[0]).astype(bf)
    row = lax.broadcasted_iota(jnp.int32, (SUBLANES, LANES), 0)
    nblk = N_KEYS // SUBLANES
    for h in range(PEER_HEADS):
        tops = []
        for pp in range(2):
            hp = h * 2 + pp
            st = _dot(subk_ref[hp], qt[hp * HALF_KEY:(hp + 1) * HALF_KEY])
            s_ref[0, hp * N_KEYS:(hp + 1) * N_KEYS, :] = st
            xs = [st[v * SUBLANES:(v + 1) * SUBLANES] for v in range(nblk)]
            _bitonic_sort_desc(xs)
            for shift in (4, 2, 1):
                rolled = [pltpu.roll(xs[nblk - 1 - v], shift, 0) for v in range(nblk)]
                xs = [jnp.maximum(a, b) for a, b in zip(xs, rolled)]
                _bitonic_merge_desc(xs)
            comp = []
            for v in range(PEER_TOPK):
                c = xs[v][:, 0:LANES]
                for gidx in range(1, groups):
                    c = jnp.where(row == gidx, xs[v][:, gidx * LANES:(gidx + 1) * LANES], c)
                comp.append(c)
            tops.append(comp)
        a, b = tops
        cands = [a[i] + b[j] for (i, j) in _CAND_PAIRS]
        cands += [jnp.full((SUBLANES, LANES), -jnp.inf, jnp.float32)] * (64 - len(cands))
        _bitonic_sort_desc(cands)
        cmax = cands[0]
        z = jnp.zeros((SUBLANES, LANES), jnp.float32)
        for kk in range(PEER_TOPK):
            z = z + jnp.exp(cands[kk] - cmax)
        for r, val in enumerate((cands[PEER_TOPK - 1], a[0], b[0], z)):
            wide = jnp.concatenate([val[gidx:gidx + 1, :] for gidx in range(groups)], axis=1)
            stats_ref[0, h * 4 + r:h * 4 + r + 1, :] = wide


def _route(xn2t, W):
    B, _, S = xn2t.shape
    tr = TR_ROUTE
    const = lambda shape: pl.BlockSpec(shape, lambda b, i: (0,) * len(shape))
    return pl.pallas_call(
        _route_kernel,
        grid=(B, S // tr),
        in_specs=[
            pl.BlockSpec((1, D_MODEL, tr), lambda b, i: (b, 0, i)),
            const((PEER_HEADS * D_KEY, D_MODEL)),
            const((PEER_HEADS * 2, N_KEYS, HALF_KEY)),
        ],
        out_specs=[
            pl.BlockSpec((1, PEER_HEADS * 2 * N_KEYS, tr), lambda b, i: (b, 0, i)),
            pl.BlockSpec((1, PEER_HEADS * 4, tr), lambda b, i: (b, 0, i)),
        ],
        out_shape=[
            jax.ShapeDtypeStruct((B, PEER_HEADS * 2 * N_KEYS, S), jnp.float32),
            jax.ShapeDtypeStruct((B, PEER_HEADS * 4, S), jnp.float32),
        ],
        compiler_params=_cparams(("parallel", "parallel")),
        name="route",
    )(xn2t, W["wpqt"], W["subk"])


def _peer_kernel(xn2t_ref, s_ref, stats_ref, ht_ref, u_ref, vt_ref, gfin_ref, y_ref,
                 e1_ref, e2_ref, acc_ref, hid_ref, a_ref, s1_ref, s2_ref, tau_ref):
    e = pl.program_id(2)
    ne = pl.num_programs(2)
    ec = u_ref.shape[0]
    tt = xn2t_ref.shape[2]
    rows_per_chunk = ec // N_KEYS

    @pl.when(e == 0)
    def _():
        for h in range(PEER_HEADS):
            amax = stats_ref[0, h * 4 + 1:h * 4 + 2, :]
            bmax = stats_ref[0, h * 4 + 2:h * 4 + 3, :]
            z = stats_ref[0, h * 4 + 3:h * 4 + 4, :]
            s1 = s_ref[0, (2 * h) * N_KEYS:(2 * h + 1) * N_KEYS, :]
            s2 = s_ref[0, (2 * h + 1) * N_KEYS:(2 * h + 2) * N_KEYS, :]
            e1_ref[h * N_KEYS:(h + 1) * N_KEYS, :] = jnp.exp(s1 - amax)
            e2_ref[h * N_KEYS:(h + 1) * N_KEYS, :] = jnp.exp(s2 - bmax) / z
            s1_ref[h * N_KEYS:(h + 1) * N_KEYS, :] = s1
            s2_ref[h * N_KEYS:(h + 1) * N_KEYS, :] = s2
            tau_ref[h:h + 1, :] = stats_ref[0, h * 4:h * 4 + 1, :]
        acc_ref[...] = jnp.zeros_like(acc_ref)

    def build(rs):
        first = [e * rows_per_chunk + r for r in rs]
        s1rows = [[s1_ref[pl.ds(h * N_KEYS + i, 1), :] for h in range(PEER_HEADS)] for i in first]
        e1rows = [[e1_ref[pl.ds(h * N_KEYS + i, 1), :] for h in range(PEER_HEADS)] for i in first]
        for t0 in range(0, tt, LANES):
            lanes = slice(t0, t0 + LANES)
            for j0 in range(0, N_KEYS, PEER_TJ):
                ws = [jnp.zeros((PEER_TJ, LANES), jnp.float32) for _ in rs]
                for h in range(PEER_HEADS):
                    tau = tau_ref[h:h + 1, lanes]
                    s2 = s2_ref[h * N_KEYS + j0:h * N_KEYS + j0 + PEER_TJ, lanes]
                    e2 = e2_ref[h * N_KEYS + j0:h * N_KEYS + j0 + PEER_TJ, lanes]
                    for k in range(len(rs)):
                        ws[k] = ws[k] + jnp.where(s1rows[k][h][:, lanes] + s2 >= tau,
                                                  e1rows[k][h][:, lanes] * e2, 0.0)
                for k, r in enumerate(rs):
                    rows = slice(r * N_KEYS + j0, r * N_KEYS + j0 + PEER_TJ)
                    hid = hid_ref[rows, lanes]
                    act = 0.5 * hid * (1.0 + lax.erf(hid * (2.0 ** -0.5)))
                    a_ref[rows, lanes] = (act * ws[k]).astype(MM_DTYPE)

    half = ec // 2
    rows_half = rows_per_chunk // 2
    xt = xn2t_ref[0]
    hid_ref[:half] = _dot(u_ref[:half], xt)
    hid_ref[half:] = _dot(u_ref[half:], xt)
    build(list(range(rows_half)))
    acc_ref[...] += _dot(vt_ref[:, :half], a_ref[:half])
    build(list(range(rows_half, rows_per_chunk)))
    acc_ref[...] += _dot(vt_ref[:, half:], a_ref[half:])

    @pl.when(e == ne - 1)
    def _():
        yt = ht_ref[0] + acc_ref[...]
        r = lax.rsqrt(jnp.mean(yt * yt, axis=0, keepdims=True) + EPS)
        y_ref[0] = (yt * r * gfin_ref[...]).T


def _peer(xn2t, s, stats, ht, W):
    B, _, S = xn2t.shape
    tt, ec = TT_PEER, EC_PEER
    return pl.pallas_call(
        _peer_kernel,
        grid=(B, S // tt, N_EXPERTS // ec),
        in_specs=[
            pl.BlockSpec((1, D_MODEL, tt), lambda b, i, e: (b, 0, i)),
            pl.BlockSpec((1, PEER_HEADS * 2 * N_KEYS, tt), lambda b, i, e: (b, 0, i)),
            pl.BlockSpec((1, PEER_HEADS * 4, tt), lambda b, i, e: (b, 0, i)),
            pl.BlockSpec((1, D_MODEL, tt), lambda b, i, e: (b, 0, i)),
            pl.BlockSpec((ec, D_MODEL), lambda b, i, e: (e, 0)),
            pl.BlockSpec((D_MODEL, ec), lambda b, i, e: (0, e)),
            pl.BlockSpec((D_MODEL, tt), lambda b, i, e: (0, 0)),
        ],
        out_specs=pl.BlockSpec((1, tt, D_MODEL), lambda b, i, e: (b, i, 0)),
        out_shape=jax.ShapeDtypeStruct((B, S, D_MODEL), jnp.float32),
        scratch_shapes=[
            pltpu.VMEM((PEER_HEADS * N_KEYS, tt), jnp.float32),
            pltpu.VMEM((PEER_HEADS * N_KEYS, tt), jnp.float32),
            pltpu.VMEM((D_MODEL, tt), jnp.float32),
            pltpu.VMEM((ec, tt), jnp.float32),
            pltpu.VMEM((ec, tt), MM_DTYPE),
            pltpu.VMEM((PEER_HEADS * N_KEYS, tt), jnp.float32),
            pltpu.VMEM((PEER_HEADS * N_KEYS, tt), jnp.float32),
            pltpu.VMEM((PEER_HEADS, tt), jnp.float32),
        ],
        compiler_params=_cparams(("parallel", "parallel", "arbitrary")),
        name="peer",
    )(xn2t, s, stats, ht, W["u"], W["vt"], W["gfin"])


def _forward(x, W):
    qat, ka, vat, qan, kan, qbt, kb, vbt, qbn, kbn = _proj(x, W)
    oat = _attention(qat, ka, vat, qan, kan, groups_per_step=1, rep=H_A // G_A, dv=HD_A, name="attn_gqa")
    obt = _attention(qbt, kb, vbt, qbn, kbn, groups_per_step=2, rep=1, dv=V_B, name="attn_mla")
    ht, xn2t = _post(x, oat, obt, W)
    s, stats = _route(xn2t, W)
    return _peer(xn2t, s, stats, ht, W)


def kernel(x_prompt, x_sample, ln1_g, w_in, qa_norm_g, ka_norm_g, cq_norm_g, ckv_norm_g, w_uq, w_ukv,
           w_br_a, w_br_b, w_out, ln2_g, w_pq, sub_keys, expert_u, expert_v, final_g):
    params = (ln1_g, w_in, qa_norm_g, ka_norm_g, cq_norm_g, ckv_norm_g, w_uq, w_ukv,
              w_br_a, w_br_b, w_out, ln2_g, w_pq, sub_keys, expert_u, expert_v, final_g)
    assert x_prompt.shape[1] == x_sample.shape[1]
    W = _prepare(x_prompt.shape[1], *params)
    return (_forward(x_prompt, W), _forward(x_sample, W))
```

```python
import functools
import math

import numpy as np
import jax
import jax.numpy as jnp
from jax import lax
from jax.experimental import pallas as pl
from jax.experimental.pallas import tpu as pltpu

D_MODEL = 1024
GRID_W = 64
ROPE_THETA = 10000.0
EPS = 1e-6
H_A, G_A, HD_A = 8, 2, 64
H_B, Q_LORA, KV_LORA, NOPE_B, ROPE_B, V_B = 8, 256, 128, 64, 32, 64
PEER_HEADS, N_KEYS, D_KEY, PEER_TOPK = 8, 128, 256, 16
N_EXPERTS = N_KEYS * N_KEYS
HALF_KEY = D_KEY // 2

LANES = 128
SUBLANES = 8
VMEM_LIMIT_BYTES = 56 * 1024 * 1024

DQ_PAD = LANES
LOG2E = math.log2(math.e)
MM_DTYPE = jnp.bfloat16
ONES_ROWS = 2 * SUBLANES
VROWS_A = HD_A + ONES_ROWS
VROWS_B = V_B + ONES_ROWS
SAFE_SCORE_BOUND = 60.0

TM_PROJ = 512
TQ_ATTN = 512
TK_ATTN = 256
ATTN_UNROLL = 16
TM_POST = 512
TR_ROUTE = SUBLANES * LANES
TT_PEER = 512
EC_PEER = 2048
PEER_TJ = 32

_R_QA, _R_QA2, _R_KA, _R_KA2, _R_VA, _R_CQ, _R_CKV, _R_KR, _R_KR2, _R_END = (
    0, 512, 1024, 1152, 1280, 1408, 1664, 1792, 1824, 1856)
_Q2_ROPE = H_B * NOPE_B
_Q2_ROPE2 = _Q2_ROPE + H_B * ROPE_B
_Q2_END = _Q2_ROPE2 + H_B * ROPE_B
_T_QAC, _T_QAS, _T_KAC, _T_KAS, _T_QBC, _T_QBS, _T_KBC, _T_KBS, _T_END = (
    0, 64, 128, 192, 256, 288, 320, 352, 384)


def _cparams(sem):
    return pltpu.CompilerParams(dimension_semantics=sem, vmem_limit_bytes=VMEM_LIMIT_BYTES)


def _dot(a, b):
    return jnp.dot(a, b, preferred_element_type=jnp.float32)


def _rope_partner(d):
    half, qtr = d // 2, d // 4
    e = np.arange(d)
    within = e % half
    first = within < qtr
    partner = np.where(first, e + qtr, e - qtr)
    sign = np.where(first, -1.0, 1.0).astype(np.float32)
    freq_idx = within % qtr
    use_col = e >= half
    return partner, sign, freq_idx, use_col


def _rope_tables(S, d):
    half = d // 2
    partner, sign, freq_idx, use_col = _rope_partner(d)
    t = jnp.arange(S, dtype=jnp.int32)
    r = (t // GRID_W).astype(jnp.float32)
    c = (t % GRID_W).astype(jnp.float32)
    freqs = ROPE_THETA ** (-jnp.arange(0, half, 2, dtype=jnp.float32) / half)
    f = freqs[freq_idx]
    pos = jnp.where(jnp.asarray(use_col)[:, None], c[None, :], r[None, :])
    ang = pos * f[:, None]
    return jnp.cos(ang), jnp.sin(ang) * jnp.asarray(sign)[:, None], partner


def _prepare(S, ln1_g, w_in, qa_norm_g, ka_norm_g, cq_norm_g, ckv_norm_g, w_uq, w_ukv,
             w_br_a, w_br_b, w_out, ln2_g, w_pq, sub_keys, expert_u, expert_v, final_g):
    bf = MM_DTYPE
    w = w_in[0]
    c_qa, c_ka, c_va, c_cq, c_ckv, c_kr, c_ga = np.cumsum(
        [H_A * HD_A, G_A * HD_A, G_A * HD_A, Q_LORA, KV_LORA, ROPE_B, D_MODEL]).tolist()
    p64, _, _, _ = _rope_partner(HD_A)
    p32, _, _, _ = _rope_partner(ROPE_B)
    qa_perm = (np.arange(H_A)[:, None] * HD_A + p64[None, :]).reshape(-1)
    ka_perm = (np.arange(G_A)[:, None] * HD_A + p64[None, :]).reshape(-1)
    w_qa, w_ka = w[:, :c_qa], w[:, c_qa:c_ka]
    w_kr = w[:, c_ckv:c_kr]
    w1 = jnp.concatenate([
        w_qa, w_qa[:, qa_perm], w_ka, w_ka[:, ka_perm], w[:, c_ka:c_va],
        w[:, c_va:c_cq], w[:, c_cq:c_ckv], w_kr, w_kr[:, p32]], axis=1)
    w1t = w1.T.astype(bf)
    wgt = w[:, c_kr:].T.astype(bf)

    uq = w_uq[0].reshape(Q_LORA, H_B, NOPE_B + ROPE_B)
    uq_nope = uq[:, :, :NOPE_B].reshape(Q_LORA, H_B * NOPE_B)
    uq_rope = uq[:, :, NOPE_B:]
    wuqt = jnp.concatenate([uq_nope, uq_rope.reshape(Q_LORA, -1),
                            uq_rope[:, :, p32].reshape(Q_LORA, -1)], axis=1).T.astype(bf)
    ukv = w_ukv[0].reshape(KV_LORA, H_B, NOPE_B + V_B)
    wukvt = jnp.concatenate([ukv[:, :, :NOPE_B].reshape(KV_LORA, -1),
                             ukv[:, :, NOPE_B:].reshape(KV_LORA, -1)], axis=1).T.astype(bf)

    cos64, sin64, _ = _rope_tables(S, HD_A)
    cos32, sin32, _ = _rope_tables(S, ROPE_B)
    sa = (HD_A ** -0.5) * LOG2E
    sb = ((NOPE_B + ROPE_B) ** -0.5) * LOG2E
    gq, gk = qa_norm_g[0], ka_norm_g[0]
    tabs = jnp.concatenate([
        cos64 * (gq * sa)[:, None], sin64 * (gq[p64] * sa)[:, None],
        cos64 * gk[:, None], sin64 * gk[p64][:, None],
        cos32 * sb, sin32 * sb, cos32, sin32], axis=0)

    def bcast(g, n):
        return jnp.broadcast_to(g.reshape(-1, 1), (g.size, n))

    return dict(
        ln1=ln1_g[0].reshape(1, D_MODEL), w1t=w1t, wgt=wgt, wuqt=wuqt, wukvt=wukvt, tabs=tabs,
        gcq=bcast(cq_norm_g[0], TM_PROJ), gckv=bcast(ckv_norm_g[0], TM_PROJ),
        wbrat=w_br_a[0].T.astype(bf), wbrbt=w_br_b[0].T.astype(bf), woutt=w_out[0].T.astype(bf),
        g2=bcast(ln2_g[0], TM_POST), wpqt=w_pq[0].T.astype(bf),
        subk=sub_keys[0].reshape(PEER_HEADS * 2, N_KEYS, HALF_KEY).astype(bf),
        u=expert_u[0].astype(bf), vt=expert_v[0].T.astype(bf),
        gfin=bcast(final_g, TT_PEER), sb=sb)


def _proj_kernel(x_ref, ln1_ref, w1t_ref, tab_ref, gcq_ref, gckv_ref, wuqt_ref, wukvt_ref,
                 qat_ref, ka_ref, vat_ref, qan_ref, kan_ref, qbt_ref, kb_ref, vbt_ref, qbn_ref, kbn_ref, *, sb):
    bf = MM_DTYPE
    x = x_ref[0]
    tm = x.shape[0]
    xn = x * lax.rsqrt(jnp.mean(x * x, axis=-1, keepdims=True) + EPS) * ln1_ref[...]
    xnt = xn.T.astype(bf)
    p = _dot(w1t_ref[...], xnt)
    tab = tab_ref[...]
    zeros64 = jnp.zeros((64, tm), jnp.float32)
    zeros32 = jnp.zeros((32, tm), jnp.float32)
    ones = jnp.ones((ONES_ROWS, tm), bf)

    def rstd(y):
        return lax.rsqrt(jnp.mean(y * y, axis=0, keepdims=True) + EPS)

    def sqnorm(yb):
        yf = yb.astype(jnp.float32)
        return jnp.sum(yf * yf, axis=0, keepdims=True)

    rep = H_A // G_A
    for h in range(H_A):
        y = p[_R_QA + h * HD_A:_R_QA + (h + 1) * HD_A]
        y2 = p[_R_QA2 + h * HD_A:_R_QA2 + (h + 1) * HD_A]
        q = ((y * tab[_T_QAC:_T_QAS] + y2 * tab[_T_QAS:_T_KAC]) * rstd(y)).astype(bf)
        qat_ref[0, h * DQ_PAD:h * DQ_PAD + HD_A, :] = q
        qat_ref[0, h * DQ_PAD + HD_A:(h + 1) * DQ_PAD, :] = zeros64.astype(bf)
        qan_ref[0, h // rep, h % rep:h % rep + 1, :] = sqnorm(q)
    for g in range(G_A):
        y = p[_R_KA + g * HD_A:_R_KA + (g + 1) * HD_A]
        y2 = p[_R_KA2 + g * HD_A:_R_KA2 + (g + 1) * HD_A]
        k = (y * tab[_T_KAC:_T_KAS] + y2 * tab[_T_KAS:_T_QBC]) * rstd(y)
        ka_ref[0, g] = jnp.concatenate([k, zeros64], axis=0).T.astype(bf)
        kan_ref[0, g, 0:1, :] = sqnorm(k.astype(bf))
        vat_ref[0, g * VROWS_A:g * VROWS_A + HD_A, :] = p[_R_VA + g * HD_A:_R_VA + (g + 1) * HD_A].astype(bf)
        vat_ref[0, g * VROWS_A + HD_A:(g + 1) * VROWS_A, :] = ones

    cq = p[_R_CQ:_R_CKV]
    cqn = (cq * rstd(cq) * gcq_ref[...]).astype(bf)
    q2 = _dot(wuqt_ref[...], cqn)
    for h in range(H_B):
        nope = (q2[h * NOPE_B:(h + 1) * NOPE_B] * sb).astype(bf)
        yr = q2[_Q2_ROPE + h * ROPE_B:_Q2_ROPE + (h + 1) * ROPE_B]
        yr2 = q2[_Q2_ROPE2 + h * ROPE_B:_Q2_ROPE2 + (h + 1) * ROPE_B]
        rope = (yr * tab[_T_QBC:_T_QBS] + yr2 * tab[_T_QBS:_T_KBC]).astype(bf)
        base = h * DQ_PAD
        qbt_ref[0, base:base + NOPE_B, :] = nope
        qbt_ref[0, base + NOPE_B:base + NOPE_B + ROPE_B, :] = rope
        qbt_ref[0, base + NOPE_B + ROPE_B:base + DQ_PAD, :] = zeros32.astype(bf)
        qbn_ref[0, h // 2, h % 2:h % 2 + 1, :] = sqnorm(nope) + sqnorm(rope)
    ckv = p[_R_CKV:_R_KR]
    ckvn = (ckv * rstd(ckv) * gckv_ref[...]).astype(bf)
    kv = _dot(wukvt_ref[...], ckvn)
    krope = p[_R_KR:_R_KR2] * tab[_T_KBC:_T_KBS] + p[_R_KR2:_R_END] * tab[_T_KBS:_T_END]
    krope_n = sqnorm(krope.astype(bf))
    for h in range(H_B):
        knope = kv[h * NOPE_B:(h + 1) * NOPE_B]
        kt = jnp.concatenate([knope, krope, zeros32], axis=0)
        kb_ref[0, h] = kt.T.astype(bf)
        kbn_ref[0, h // 2, h % 2:h % 2 + 1, :] = sqnorm(knope.astype(bf)) + krope_n
        v0 = H_B * NOPE_B + h * V_B
        vbt_ref[0, h * VROWS_B:h * VROWS_B + V_B, :] = kv[v0:v0 + V_B].astype(bf)
        vbt_ref[0, h * VROWS_B + V_B:(h + 1) * VROWS_B, :] = ones


def _proj(x, W):
    B, S, _ = x.shape
    tm = TM_PROJ
    bf = MM_DTYPE
    f32 = jnp.float32
    rep = H_A // G_A
    const = lambda shape: pl.BlockSpec(shape, lambda b, i: (0,) * len(shape))
    return pl.pallas_call(
        functools.partial(_proj_kernel, sb=W["sb"]),
        grid=(B, S // tm),
        in_specs=[
            pl.BlockSpec((1, tm, D_MODEL), lambda b, i: (b, i, 0)),
            const((1, D_MODEL)), const((_R_END, D_MODEL)),
            pl.BlockSpec((_T_END, tm), lambda b, i: (0, i)),
            const((Q_LORA, tm)), const((KV_LORA, tm)),
            const((_Q2_END, Q_LORA)), const((H_B * (NOPE_B + V_B), KV_LORA)),
        ],
        out_specs=[
            pl.BlockSpec((1, H_A * DQ_PAD, tm), lambda b, i: (b, 0, i)),
            pl.BlockSpec((1, G_A, tm, DQ_PAD), lambda b, i: (b, 0, i, 0)),
            pl.BlockSpec((1, G_A * VROWS_A, tm), lambda b, i: (b, 0, i)),
            pl.BlockSpec((1, G_A, rep, tm), lambda b, i: (b, 0, 0, i)),
            pl.BlockSpec((1, G_A, 1, tm), lambda b, i: (b, 0, 0, i)),
            pl.BlockSpec((1, H_B * DQ_PAD, tm), lambda b, i: (b, 0, i)),
            pl.BlockSpec((1, H_B, tm, DQ_PAD), lambda b, i: (b, 0, i, 0)),
            pl.BlockSpec((1, H_B * VROWS_B, tm), lambda b, i: (b, 0, i)),
            pl.BlockSpec((1, H_B // 2, 2, tm), lambda b, i: (b, 0, 0, i)),
            pl.BlockSpec((1, H_B // 2, 2, tm), lambda b, i: (b, 0, 0, i)),
        ],
        out_shape=[
            jax.ShapeDtypeStruct((B, H_A * DQ_PAD, S), bf),
            jax.ShapeDtypeStruct((B, G_A, S, DQ_PAD), bf),
            jax.ShapeDtypeStruct((B, G_A * VROWS_A, S), bf),
            jax.ShapeDtypeStruct((B, G_A, rep, S), f32),
            jax.ShapeDtypeStruct((B, G_A, 1, S), f32),
            jax.ShapeDtypeStruct((B, H_B * DQ_PAD, S), bf),
            jax.ShapeDtypeStruct((B, H_B, S, DQ_PAD), bf),
            jax.ShapeDtypeStruct((B, H_B * VROWS_B, S), bf),
            jax.ShapeDtypeStruct((B, H_B // 2, 2, S), f32),
            jax.ShapeDtypeStruct((B, H_B // 2, 2, S), f32),
        ],
        compiler_params=_cparams(("parallel", "parallel")),
        name="proj",
    )(x, W["ln1"], W["w1t"], W["tabs"], W["gcq"], W["gckv"], W["wuqt"], W["wukvt"])


def _attn_kernel(qt_ref, k_ref, vt_ref, qn_ref, kn_ref, ot_ref, s_ref, p_ref, *, heads, rep, dv, tk, heads_per_loop):
    S = k_ref.shape[2]
    tq = qt_ref.shape[2]
    nk = S // tk
    vrows = dv + ONES_ROWS

    def scores(h, c):
        off = pl.multiple_of(c * tk, tk)
        kc = k_ref[0, h // rep, pl.ds(off, tk), :]
        return _dot(kc, qt_ref[0, h * DQ_PAD:(h + 1) * DQ_PAD, :])

    def vchunk(h, c):
        off = pl.multiple_of(c * tk, tk)
        g = h // rep
        return vt_ref[0, g * vrows:(g + 1) * vrows, pl.ds(off, tk)]

    def finish(h, acc):
        ot_ref[0, h * dv:(h + 1) * dv, :] = (acc[:dv] / acc[dv:dv + 1]).astype(ot_ref.dtype)

    zero_acc = lambda: jnp.zeros((vrows, tq), jnp.float32)
    bound2 = jnp.max(qn_ref[0, 0]) * jnp.max(kn_ref[0, 0])
    safe = bound2 <= SAFE_SCORE_BOUND * SAFE_SCORE_BOUND

    def probs(s, m, with_max):
        if not with_max:
            return m, None, jnp.exp2(s).astype(MM_DTYPE)
        mn = jnp.maximum(m, jnp.max(s, axis=0, keepdims=True))
        return mn, jnp.exp2(m - mn), jnp.exp2(s - mn).astype(MM_DTYPE)

    def accumulate(h, c, p_slot, alpha, acc):
        pv = _dot(vchunk(h, c), p_slot[...])
        return pv + (acc if alpha is None else alpha * acc)

    def run(with_max):
        for h0 in range(0, heads, heads_per_loop):
            hs = list(range(h0, h0 + heads_per_loop))
            for i, h in enumerate(hs):
                s_ref[0, i] = scores(h, 0)
                p_ref[1, i] = jnp.zeros((tk, tq), MM_DTYPE)

            def body(j, carry, hs=hs):
                c0 = 2 * j
                prev = jnp.maximum(c0 - 1, 0)
                nxt = jnp.minimum(c0 + 2, nk - 1)
                out = []
                for i, (h, (m, alpha_b, acc)) in enumerate(zip(hs, carry)):
                    s_ref[1, i] = scores(h, c0 + 1)
                    acc = accumulate(h, prev, p_ref.at[1, i], alpha_b, acc)
                    m, alpha_a, pa = probs(s_ref[0, i], m, with_max)
                    p_ref[0, i] = pa
                    acc = accumulate(h, c0, p_ref.at[0, i], alpha_a, acc)
                    s_ref[0, i] = scores(h, nxt)
                    m, alpha_b, pb = probs(s_ref[1, i], m, with_max)
                    p_ref[1, i] = pb
                    out.append((m, alpha_b, acc))
                return tuple(out)

            one = jnp.ones((1, tq), jnp.float32) if with_max else None
            init = tuple((jnp.full((1, tq), -jnp.inf, jnp.float32), one, zero_acc()) for _ in hs)
            res = lax.fori_loop(0, nk // 2, body, init, unroll=ATTN_UNROLL)
            for i, ((m, alpha_b, acc), h) in enumerate(zip(res, hs)):
                finish(h, accumulate(h, nk - 1, p_ref.at[1, i], alpha_b, acc))

    pl.when(safe)(lambda: run(False))
    pl.when(jnp.logical_not(safe))(lambda: run(True))


def _attention(qt, k, vt, qn, kn, *, groups_per_step, rep, dv, name):
    B, _, S = qt.shape
    G = k.shape[1]
    gs = groups_per_step
    heads = gs * rep
    tq = TQ_ATTN
    hpl = min(2, heads)
    vrows = dv + ONES_ROWS
    assert (S // TK_ATTN) % 2 == 0
    kern = functools.partial(_attn_kernel, heads=heads, rep=rep, dv=dv, tk=TK_ATTN, heads_per_loop=hpl)
    return pl.pallas_call(
        kern,
        grid=(B, G // gs, S // tq),
        in_specs=[
            pl.BlockSpec((1, heads * DQ_PAD, tq), lambda b, g, i: (b, g, i)),
            pl.BlockSpec((1, gs, S, DQ_PAD), lambda b, g, i: (b, g, 0, 0)),
            pl.BlockSpec((1, gs * vrows, S), lambda b, g, i: (b, g, 0)),
            pl.BlockSpec((1, 1, heads, tq), lambda b, g, i: (b, g, 0, i)),
            pl.BlockSpec((1, 1, gs, S), lambda b, g, i: (b, g, 0, 0)),
        ],
        out_specs=pl.BlockSpec((1, heads * dv, tq), lambda b, g, i: (b, g, i)),
        out_shape=jax.ShapeDtypeStruct((B, G * rep * dv, S), MM_DTYPE),
        scratch_shapes=[pltpu.VMEM((2, hpl, TK_ATTN, tq), jnp.float32),
                        pltpu.VMEM((2, hpl, TK_ATTN, tq), MM_DTYPE)],
        compiler_params=_cparams(("parallel", "parallel", "arbitrary")),
        name=name,
    )(qt, k, vt, qn, kn)


def _post_kernel(x_ref, ln1_ref, wgt_ref, oat_ref, obt_ref, wbrat_ref, wbrbt_ref, woutt_ref, g2_ref,
                 ht_ref, xn2t_ref):
    bf = MM_DTYPE
    x = x_ref[0]
    xn = x * lax.rsqrt(jnp.mean(x * x, axis=-1, keepdims=True) + EPS) * ln1_ref[...]
    xnt = xn.T.astype(bf)
    gates = jax.nn.sigmoid(_dot(wgt_ref[...], xnt))
    ma = _dot(wbrat_ref[...], oat_ref[0])
    mb = _dot(wbrbt_ref[...], obt_ref[0])
    merged = gates[:D_MODEL] * ma + gates[D_MODEL:] * mb
    ht = x.T + _dot(woutt_ref[...], merged.astype(bf))
    ht_ref[0] = ht
    r2 = lax.rsqrt(jnp.mean(ht * ht, axis=0, keepdims=True) + EPS)
    xn2t_ref[0] = (ht * r2 * g2_ref[...]).astype(bf)


def _post(x, oat, obt, W):
    B, S, _ = x.shape
    tm = TM_POST
    const = lambda shape: pl.BlockSpec(shape, lambda b, i: (0,) * len(shape))
    return pl.pallas_call(
        _post_kernel,
        grid=(B, S // tm),
        in_specs=[
            pl.BlockSpec((1, tm, D_MODEL), lambda b, i: (b, i, 0)),
            const((1, D_MODEL)), const((2 * D_MODEL, D_MODEL)),
            pl.BlockSpec((1, H_A * HD_A, tm), lambda b, i: (b, 0, i)),
            pl.BlockSpec((1, H_B * V_B, tm), lambda b, i: (b, 0, i)),
            const((D_MODEL, H_A * HD_A)), const((D_MODEL, H_B * V_B)), const((D_MODEL, D_MODEL)),
            const((D_MODEL, tm)),
        ],
        out_specs=[
            pl.BlockSpec((1, D_MODEL, tm), lambda b, i: (b, 0, i)),
            pl.BlockSpec((1, D_MODEL, tm), lambda b, i: (b, 0, i)),
        ],
        out_shape=[
            jax.ShapeDtypeStruct((B, D_MODEL, S), jnp.float32),
            jax.ShapeDtypeStruct((B, D_MODEL, S), MM_DTYPE),
        ],
        compiler_params=_cparams(("parallel", "parallel")),
        name="post",
    )(x, W["ln1"], W["wgt"], oat, obt, W["wbrat"], W["wbrbt"], W["woutt"], W["g2"])


def _cmpx(xs, i, j):
    a, b = xs[i], xs[j]
    xs[i], xs[j] = jnp.maximum(a, b), jnp.minimum(a, b)


def _bitonic_merge_desc(xs):
    n = len(xs)
    j = n // 2
    while j >= 1:
        for i in range(n):
            l = i ^ j
            if l > i:
                _cmpx(xs, i, l)
        j //= 2


def _bitonic_sort_desc(xs):
    n = len(xs)
    k = 2
    while k <= n:
        j = k // 2
        while j >= 1:
            for i in range(n):
                l = i ^ j
                if l > i:
                    if (i & k) == 0:
                        _cmpx(xs, i, l)
                    else:
                        _cmpx(xs, l, i)
            j //= 2
        k *= 2


_CAND_PAIRS = [(i, j) for i in range(PEER_TOPK) for j in range(PEER_TOPK) if (i + 1) * (j + 1) <= PEER_TOPK]


def _route_kernel(xn2t_ref, wpqt_ref, subk_ref, s_ref, stats_ref):
    bf = MM_DTYPE
    tr = xn2t_ref.shape[2]
    groups = tr // LANES
    qt = _dot(wpqt_ref[...], xn2t_ref[0]).astype(bf)
    row = lax.broadcasted_iota(jnp.int32, (SUBLANES, LANES), 0)
    nblk = N_KEYS // SUBLANES
    for h in range(PEER_HEADS):
        tops = []
        for pp in range(2):
            hp = h * 2 + pp
            st = _dot(subk_ref[hp], qt[hp * HALF_KEY:(hp + 1) * HALF_KEY])
            s_ref[0, hp * N_KEYS:(hp + 1) * N_KEYS, :] = st
            xs = [st[v * SUBLANES:(v + 1) * SUBLANES] for v in range(nblk)]
            _bitonic_sort_desc(xs)
            for shift in (4, 2, 1):
                rolled = [pltpu.roll(xs[nblk - 1 - v], shift, 0) for v in range(nblk)]
                xs = [jnp.maximum(a, b) for a, b in zip(xs, rolled)]
                _bitonic_merge_desc(xs)
            comp = []
            for v in range(PEER_TOPK):
                c = xs[v][:, 0:LANES]
                for gidx in range(1, groups):
                    c = jnp.where(row == gidx, xs[v][:, gidx * LANES:(gidx + 1) * LANES], c)
                comp.append(c)
            tops.append(comp)
        a, b = tops
        cands = [a[i] + b[j] for (i, j) in _CAND_PAIRS]
        cands += [jnp.full((SUBLANES, LANES), -jnp.inf, jnp.float32)] * (64 - len(cands))
        _bitonic_sort_desc(cands)
        cmax = cands[0]
        z = jnp.zeros((SUBLANES, LANES), jnp.float32)
        for kk in range(PEER_TOPK):
            z = z + jnp.exp(cands[kk] - cmax)
        for r, val in enumerate((cands[PEER_TOPK - 1], a[0], b[0], z)):
            wide = jnp.concatenate([val[gidx:gidx + 1, :] for gidx in range(groups)], axis=1)
            stats_ref[0, h * 4 + r:h * 4 + r + 1, :] = wide


def _route(xn2t, W):
    B, _, S = xn2t.shape
    tr = TR_ROUTE
    const = lambda shape: pl.BlockSpec(shape, lambda b, i: (0,) * len(shape))
    return pl.pallas_call(
        _route_kernel,
        grid=(B, S // tr),
        in_specs=[
            pl.BlockSpec((1, D_MODEL, tr), lambda b, i: (b, 0, i)),
            const((PEER_HEADS * D_KEY, D_MODEL)),
            const((PEER_HEADS * 2, N_KEYS, HALF_KEY)),
        ],
        out_specs=[
            pl.BlockSpec((1, PEER_HEADS * 2 * N_KEYS, tr), lambda b, i: (b, 0, i)),
            pl.BlockSpec((1, PEER_HEADS * 4, tr), lambda b, i: (b, 0, i)),
        ],
        out_shape=[
            jax.ShapeDtypeStruct((B, PEER_HEADS * 2 * N_KEYS, S), jnp.float32),
            jax.ShapeDtypeStruct((B, PEER_HEADS * 4, S), jnp.float32),
        ],
        compiler_params=_cparams(("parallel", "parallel")),
        name="route",
    )(xn2t, W["wpqt"], W["subk"])


def _peer_kernel(xn2t_ref, s_ref, stats_ref, ht_ref, u_ref, vt_ref, gfin_ref, y_ref,
                 e1_ref, e2_ref, acc_ref, hid_ref, a_ref, s1_ref, s2_ref, tau_ref):
    e = pl.program_id(2)
    ne = pl.num_programs(2)
    ec = u_ref.shape[0]
    tt = xn2t_ref.shape[2]
    rows_per_chunk = ec // N_KEYS

    @pl.when(e == 0)
    def _():
        for h in range(PEER_HEADS):
            amax = stats_ref[0, h * 4 + 1:h * 4 + 2, :]
            bmax = stats_ref[0, h * 4 + 2:h * 4 + 3, :]
            z = stats_ref[0, h * 4 + 3:h * 4 + 4, :]
            s1 = s_ref[0, (2 * h) * N_KEYS:(2 * h + 1) * N_KEYS, :]
            s2 = s_ref[0, (2 * h + 1) * N_KEYS:(2 * h + 2) * N_KEYS, :]
            e1_ref[h * N_KEYS:(h + 1) * N_KEYS, :] = jnp.exp(s1 - amax)
            e2_ref[h * N_KEYS:(h + 1) * N_KEYS, :] = jnp.exp(s2 - bmax) / z
            s1_ref[h * N_KEYS:(h + 1) * N_KEYS, :] = s1
            s2_ref[h * N_KEYS:(h + 1) * N_KEYS, :] = s2
            tau_ref[h:h + 1, :] = stats_ref[0, h * 4:h * 4 + 1, :]
        acc_ref[...] = jnp.zeros_like(acc_ref)

    def build(rs):
        first = [e * rows_per_chunk + r for r in rs]
        s1rows = [[s1_ref[pl.ds(h * N_KEYS + i, 1), :] for h in range(PEER_HEADS)] for i in first]
        e1rows = [[e1_ref[pl.ds(h * N_KEYS + i, 1), :] for h in range(PEER_HEADS)] for i in first]
        for t0 in range(0, tt, LANES):
            lanes = slice(t0, t0 + LANES)
            for j0 in range(0, N_KEYS, PEER_TJ):
                ws = [jnp.zeros((PEER_TJ, LANES), jnp.float32) for _ in rs]
                for h in range(PEER_HEADS):
                    tau = tau_ref[h:h + 1, lanes]
                    s2 = s2_ref[h * N_KEYS + j0:h * N_KEYS + j0 + PEER_TJ, lanes]
                    e2 = e2_ref[h * N_KEYS + j0:h * N_KEYS + j0 + PEER_TJ, lanes]
                    for k in range(len(rs)):
                        ws[k] = ws[k] + jnp.where(s1rows[k][h][:, lanes] + s2 >= tau,
                                                  e1rows[k][h][:, lanes] * e2, 0.0)
                for k, r in enumerate(rs):
                    rows = slice(r * N_KEYS + j0, r * N_KEYS + j0 + PEER_TJ)
                    hid = hid_ref[rows, lanes]
                    act = 0.5 * hid * (1.0 + lax.erf(hid * (2.0 ** -0.5)))
                    a_ref[rows, lanes] = (act * ws[k]).astype(MM_DTYPE)

    half = ec // 2
    rows_half = rows_per_chunk // 2
    xt = xn2t_ref[0]
    hid_ref[:half] = _dot(u_ref[:half], xt)
    hid_ref[half:] = _dot(u_ref[half:], xt)
    build(list(range(rows_half)))
    acc_ref[...] += _dot(vt_ref[:, :half], a_ref[:half])
    build(list(range(rows_half, rows_per_chunk)))
    acc_ref[...] += _dot(vt_ref[:, half:], a_ref[half:])

    @pl.when(e == ne - 1)
    def _():
        yt = ht_ref[0] + acc_ref[...]
        r = lax.rsqrt(jnp.mean(yt * yt, axis=0, keepdims=True) + EPS)
        y_ref[0] = (yt * r * gfin_ref[...]).T


def _peer(xn2t, s, stats, ht, W):
    B, _, S = xn2t.shape
    tt, ec = TT_PEER, EC_PEER
    return pl.pallas_call(
        _peer_kernel,
        grid=(B, S // tt, N_EXPERTS // ec),
        in_specs=[
            pl.BlockSpec((1, D_MODEL, tt), lambda b, i, e: (b, 0, i)),
            pl.BlockSpec((1, PEER_HEADS * 2 * N_KEYS, tt), lambda b, i, e: (b, 0, i)),
            pl.BlockSpec((1, PEER_HEADS * 4, tt), lambda b, i, e: (b, 0, i)),
            pl.BlockSpec((1, D_MODEL, tt), lambda b, i, e: (b, 0, i)),
            pl.BlockSpec((ec, D_MODEL), lambda b, i, e: (e, 0)),
            pl.BlockSpec((D_MODEL, ec), lambda b, i, e: (0, e)),
            pl.BlockSpec((D_MODEL, tt), lambda b, i, e: (0, 0)),
        ],
        out_specs=pl.BlockSpec((1, tt, D_MODEL), lambda b, i, e: (b, i, 0)),
        out_shape=jax.ShapeDtypeStruct((B, S, D_MODEL), jnp.float32),
        scratch_shapes=[
            pltpu.VMEM((PEER_HEADS * N_KEYS, tt), jnp.float32),
            pltpu.VMEM((PEER_HEADS * N_KEYS, tt), jnp.float32),
            pltpu.VMEM((D_MODEL, tt), jnp.float32),
            pltpu.VMEM((ec, tt), jnp.float32),
            pltpu.VMEM((ec, tt), MM_DTYPE),
            pltpu.VMEM((PEER_HEADS * N_KEYS, tt), jnp.float32),
            pltpu.VMEM((PEER_HEADS * N_KEYS, tt), jnp.float32),
            pltpu.VMEM((PEER_HEADS, tt), jnp.float32),
        ],
        compiler_params=_cparams(("parallel", "parallel", "arbitrary")),
        name="peer",
    )(xn2t, s, stats, ht, W["u"], W["vt"], W["gfin"])


def _forward(x, W):
    qat, ka, vat, qan, kan, qbt, kb, vbt, qbn, kbn = _proj(x, W)
    oat = _attention(qat, ka, vat, qan, kan, groups_per_step=1, rep=H_A // G_A, dv=HD_A, name="attn_gqa")
    obt = _attention(qbt, kb, vbt, qbn, kbn, groups_per_step=2, rep=1, dv=V_B, name="attn_mla")
    ht, xn2t = _post(x, oat, obt, W)
    s, stats = _route(xn2t, W)
    return _peer(xn2t, s, stats, ht, W)


def kernel(x_prompt, x_sample, ln1_g, w_in, qa_norm_g, ka_norm_g, cq_norm_g, ckv_norm_g, w_uq, w_ukv,
           w_br_a, w_br_b, w_out, ln2_g, w_pq, sub_keys, expert_u, expert_v, final_g):
    params = (ln1_g, w_in, qa_norm_g, ka_norm_g, cq_norm_g, ckv_norm_g, w_uq, w_ukv,
              w_br_a, w_br_b, w_out, ln2_g, w_pq, sub_keys, expert_u, expert_v, final_g)
    assert x_prompt.shape[1] == x_sample.shape[1]
    W = _prepare(x_prompt.shape[1], *params)
    return (_forward(x_prompt, W), _forward(x_sample, W))
```

```python
import functools
import math

import numpy as np
import jax
import jax.numpy as jnp
from jax import lax
from jax.experimental import pallas as pl
from jax.experimental.pallas import tpu as pltpu

D_MODEL = 1024
GRID_W = 64
ROPE_THETA = 10000.0
EPS = 1e-6
H_A, G_A, HD_A = 8, 2, 64
H_B, Q_LORA, KV_LORA, NOPE_B, ROPE_B, V_B = 8, 256, 128, 64, 32, 64
PEER_HEADS, N_KEYS, D_KEY, PEER_TOPK = 8, 128, 256, 16
N_EXPERTS = N_KEYS * N_KEYS
HALF_KEY = D_KEY // 2

LANES = 128
SUBLANES = 8
VMEM_LIMIT_BYTES = 56 * 1024 * 1024

DQ_PAD = LANES
LOG2E = math.log2(math.e)
MM_DTYPE = jnp.bfloat16
ONES_ROWS = 2 * SUBLANES
VROWS_A = HD_A + ONES_ROWS
VROWS_B = V_B + ONES_ROWS
SAFE_SCORE_BOUND = 60.0

TM_PROJ = 512
TQ_ATTN = 512
TK_ATTN = 256
ATTN_UNROLL = 16
TM_POST = 512
TR_ROUTE = SUBLANES * LANES
TT_PEER = 512
EC_PEER = 2048
PEER_PARTS = 4
PEER_TJ = 32

_R_QA, _R_QA2, _R_KA, _R_KA2, _R_VA, _R_CQ, _R_CKV, _R_KR, _R_KR2, _R_END = (
    0, 512, 1024, 1152, 1280, 1408, 1664, 1792, 1824, 1856)
_Q2_ROPE = H_B * NOPE_B
_Q2_ROPE2 = _Q2_ROPE + H_B * ROPE_B
_Q2_END = _Q2_ROPE2 + H_B * ROPE_B
_T_QAC, _T_QAS, _T_KAC, _T_KAS, _T_QBC, _T_QBS, _T_KBC, _T_KBS, _T_END = (
    0, 64, 128, 192, 256, 288, 320, 352, 384)


def _cparams(sem):
    return pltpu.CompilerParams(dimension_semantics=sem, vmem_limit_bytes=VMEM_LIMIT_BYTES)


def _dot(a, b):
    return jnp.dot(a, b, preferred_element_type=jnp.float32)


def _rope_partner(d):
    half, qtr = d // 2, d // 4
    e = np.arange(d)
    within = e % half
    first = within < qtr
    partner = np.where(first, e + qtr, e - qtr)
    sign = np.where(first, -1.0, 1.0).astype(np.float32)
    freq_idx = within % qtr
    use_col = e >= half
    return partner, sign, freq_idx, use_col


def _rope_tables(S, d):
    half = d // 2
    partner, sign, freq_idx, use_col = _rope_partner(d)
    t = jnp.arange(S, dtype=jnp.int32)
    r = (t // GRID_W).astype(jnp.float32)
    c = (t % GRID_W).astype(jnp.float32)
    freqs = ROPE_THETA ** (-jnp.arange(0, half, 2, dtype=jnp.float32) / half)
    f = freqs[freq_idx]
    pos = jnp.where(jnp.asarray(use_col)[:, None], c[None, :], r[None, :])
    ang = pos * f[:, None]
    return jnp.cos(ang), jnp.sin(ang) * jnp.asarray(sign)[:, None], partner


def _prepare(S, ln1_g, w_in, qa_norm_g, ka_norm_g, cq_norm_g, ckv_norm_g, w_uq, w_ukv,
             w_br_a, w_br_b, w_out, ln2_g, w_pq, sub_keys, expert_u, expert_v, final_g):
    bf = MM_DTYPE
    w = w_in[0]
    c_qa, c_ka, c_va, c_cq, c_ckv, c_kr, c_ga = np.cumsum(
        [H_A * HD_A, G_A * HD_A, G_A * HD_A, Q_LORA, KV_LORA, ROPE_B, D_MODEL]).tolist()
    p64, _, _, _ = _rope_partner(HD_A)
    p32, _, _, _ = _rope_partner(ROPE_B)
    qa_perm = (np.arange(H_A)[:, None] * HD_A + p64[None, :]).reshape(-1)
    ka_perm = (np.arange(G_A)[:, None] * HD_A + p64[None, :]).reshape(-1)
    w_qa, w_ka = w[:, :c_qa], w[:, c_qa:c_ka]
    w_kr = w[:, c_ckv:c_kr]
    w1 = jnp.concatenate([
        w_qa, w_qa[:, qa_perm], w_ka, w_ka[:, ka_perm], w[:, c_ka:c_va],
        w[:, c_va:c_cq], w[:, c_cq:c_ckv], w_kr, w_kr[:, p32]], axis=1)
    w1t = w1.T.astype(bf)
    wgt = w[:, c_kr:].T.astype(bf)

    uq = w_uq[0].reshape(Q_LORA, H_B, NOPE_B + ROPE_B)
    uq_nope = uq[:, :, :NOPE_B].reshape(Q_LORA, H_B * NOPE_B)
    uq_rope = uq[:, :, NOPE_B:]
    wuqt = jnp.concatenate([uq_nope, uq_rope.reshape(Q_LORA, -1),
                            uq_rope[:, :, p32].reshape(Q_LORA, -1)], axis=1).T.astype(bf)
    ukv = w_ukv[0].reshape(KV_LORA, H_B, NOPE_B + V_B)
    wukvt = jnp.concatenate([ukv[:, :, :NOPE_B].reshape(KV_LORA, -1),
                             ukv[:, :, NOPE_B:].reshape(KV_LORA, -1)], axis=1).T.astype(bf)

    cos64, sin64, _ = _rope_tables(S, HD_A)
    cos32, sin32, _ = _rope_tables(S, ROPE_B)
    sa = (HD_A ** -0.5) * LOG2E
    sb = ((NOPE_B + ROPE_B) ** -0.5) * LOG2E
    gq, gk = qa_norm_g[0], ka_norm_g[0]
    tabs = jnp.concatenate([
        cos64 * (gq * sa)[:, None], sin64 * (gq[p64] * sa)[:, None],
        cos64 * gk[:, None], sin64 * gk[p64][:, None],
        cos32 * sb, sin32 * sb, cos32, sin32], axis=0)

    def bcast(g, n):
        return jnp.broadcast_to(g.reshape(-1, 1), (g.size, n))

    return dict(
        ln1=ln1_g[0].reshape(1, D_MODEL), w1t=w1t, wgt=wgt, wuqt=wuqt, wukvt=wukvt, tabs=tabs,
        gcq=bcast(cq_norm_g[0], TM_PROJ), gckv=bcast(ckv_norm_g[0], TM_PROJ),
        wbrat=w_br_a[0].T.astype(bf), wbrbt=w_br_b[0].T.astype(bf), woutt=w_out[0].T.astype(bf),
        g2=bcast(ln2_g[0], TM_POST), wpqt=w_pq[0].T.astype(bf),
        subk=sub_keys[0].reshape(PEER_HEADS * 2, N_KEYS, HALF_KEY).astype(bf),
        u=expert_u[0].astype(bf), vt=expert_v[0].T.astype(bf),
        gfin=bcast(final_g, TT_PEER), sb=sb)


def _proj_kernel(x_ref, ln1_ref, w1t_ref, tab_ref, gcq_ref, gckv_ref, wuqt_ref, wukvt_ref,
                 qat_ref, ka_ref, vat_ref, qan_ref, kan_ref, qbt_ref, kb_ref, vbt_ref, qbn_ref, kbn_ref, *, sb):
    bf = MM_DTYPE
    x = x_ref[0]
    tm = x.shape[0]
    xn = x * lax.rsqrt(jnp.mean(x * x, axis=-1, keepdims=True) + EPS) * ln1_ref[...]
    xnt = xn.T.astype(bf)
    p = _dot(w1t_ref[...], xnt)
    tab = tab_ref[...]
    zeros64 = jnp.zeros((64, tm), jnp.float32)
    zeros32 = jnp.zeros((32, tm), jnp.float32)
    ones = jnp.ones((ONES_ROWS, tm), bf)

    def rstd(y):
        return lax.rsqrt(jnp.mean(y * y, axis=0, keepdims=True) + EPS)

    def sqnorm(yb):
        yf = yb.astype(jnp.float32)
        return jnp.sum(yf * yf, axis=0, keepdims=True)

    rep = H_A // G_A
    for h in range(H_A):
        y = p[_R_QA + h * HD_A:_R_QA + (h + 1) * HD_A]
        y2 = p[_R_QA2 + h * HD_A:_R_QA2 + (h + 1) * HD_A]
        q = ((y * tab[_T_QAC:_T_QAS] + y2 * tab[_T_QAS:_T_KAC]) * rstd(y)).astype(bf)
        qat_ref[0, h * DQ_PAD:h * DQ_PAD + HD_A, :] = q
        qat_ref[0, h * DQ_PAD + HD_A:(h + 1) * DQ_PAD, :] = zeros64.astype(bf)
        qan_ref[0, h // rep, h % rep:h % rep + 1, :] = sqnorm(q)
    for g in range(G_A):
        y = p[_R_KA + g * HD_A:_R_KA + (g + 1) * HD_A]
        y2 = p[_R_KA2 + g * HD_A:_R_KA2 + (g + 1) * HD_A]
        k = (y * tab[_T_KAC:_T_KAS] + y2 * tab[_T_KAS:_T_QBC]) * rstd(y)
        ka_ref[0, g] = jnp.concatenate([k, zeros64], axis=0).T.astype(bf)
        kan_ref[0, g, 0:1, :] = sqnorm(k.astype(bf))
        vat_ref[0, g * VROWS_A:g * VROWS_A + HD_A, :] = p[_R_VA + g * HD_A:_R_VA + (g + 1) * HD_A].astype(bf)
        vat_ref[0, g * VROWS_A + HD_A:(g + 1) * VROWS_A, :] = ones

    cq = p[_R_CQ:_R_CKV]
    cqn = (cq * rstd(cq) * gcq_ref[...]).astype(bf)
    q2 = _dot(wuqt_ref[...], cqn)
    for h in range(H_B):
        nope = (q2[h * NOPE_B:(h + 1) * NOPE_B] * sb).astype(bf)
        yr = q2[_Q2_ROPE + h * ROPE_B:_Q2_ROPE + (h + 1) * ROPE_B]
        yr2 = q2[_Q2_ROPE2 + h * ROPE_B:_Q2_ROPE2 + (h + 1) * ROPE_B]
        rope = (yr * tab[_T_QBC:_T_QBS] + yr2 * tab[_T_QBS:_T_KBC]).astype(bf)
        base = h * DQ_PAD
        qbt_ref[0, base:base + NOPE_B, :] = nope
        qbt_ref[0, base + NOPE_B:base + NOPE_B + ROPE_B, :] = rope
        qbt_ref[0, base + NOPE_B + ROPE_B:base + DQ_PAD, :] = zeros32.astype(bf)
        qbn_ref[0, h // 2, h % 2:h % 2 + 1, :] = sqnorm(nope) + sqnorm(rope)
    ckv = p[_R_CKV:_R_KR]
    ckvn = (ckv * rstd(ckv) * gckv_ref[...]).astype(bf)
    kv = _dot(wukvt_ref[...], ckvn)
    krope = p[_R_KR:_R_KR2] * tab[_T_KBC:_T_KBS] + p[_R_KR2:_R_END] * tab[_T_KBS:_T_END]
    krope_n = sqnorm(krope.astype(bf))
    for h in range(H_B):
        knope = kv[h * NOPE_B:(h + 1) * NOPE_B]
        kt = jnp.concatenate([knope, krope, zeros32], axis=0)
        kb_ref[0, h] = kt.T.astype(bf)
        kbn_ref[0, h // 2, h % 2:h % 2 + 1, :] = sqnorm(knope.astype(bf)) + krope_n
        v0 = H_B * NOPE_B + h * V_B
        vbt_ref[0, h * VROWS_B:h * VROWS_B + V_B, :] = kv[v0:v0 + V_B].astype(bf)
        vbt_ref[0, h * VROWS_B + V_B:(h + 1) * VROWS_B, :] = ones


def _proj(x, W):
    B, S, _ = x.shape
    tm = TM_PROJ
    bf = MM_DTYPE
    f32 = jnp.float32
    rep = H_A // G_A
    const = lambda shape: pl.BlockSpec(shape, lambda b, i: (0,) * len(shape))
    return pl.pallas_call(
        functools.partial(_proj_kernel, sb=W["sb"]),
        grid=(B, S // tm),
        in_specs=[
            pl.BlockSpec((1, tm, D_MODEL), lambda b, i: (b, i, 0)),
            const((1, D_MODEL)), const((_R_END, D_MODEL)),
            pl.BlockSpec((_T_END, tm), lambda b, i: (0, i)),
            const((Q_LORA, tm)), const((KV_LORA, tm)),
            const((_Q2_END, Q_LORA)), const((H_B * (NOPE_B + V_B), KV_LORA)),
        ],
        out_specs=[
            pl.BlockSpec((1, H_A * DQ_PAD, tm), lambda b, i: (b, 0, i)),
            pl.BlockSpec((1, G_A, tm, DQ_PAD), lambda b, i: (b, 0, i, 0)),
            pl.BlockSpec((1, G_A * VROWS_A, tm), lambda b, i: (b, 0, i)),
            pl.BlockSpec((1, G_A, rep, tm), lambda b, i: (b, 0, 0, i)),
            pl.BlockSpec((1, G_A, 1, tm), lambda b, i: (b, 0, 0, i)),
            pl.BlockSpec((1, H_B * DQ_PAD, tm), lambda b, i: (b, 0, i)),
            pl.BlockSpec((1, H_B, tm, DQ_PAD), lambda b, i: (b, 0, i, 0)),
            pl.BlockSpec((1, H_B * VROWS_B, tm), lambda b, i: (b, 0, i)),
            pl.BlockSpec((1, H_B // 2, 2, tm), lambda b, i: (b, 0, 0, i)),
            pl.BlockSpec((1, H_B // 2, 2, tm), lambda b, i: (b, 0, 0, i)),
        ],
        out_shape=[
            jax.ShapeDtypeStruct((B, H_A * DQ_PAD, S), bf),
            jax.ShapeDtypeStruct((B, G_A, S, DQ_PAD), bf),
            jax.ShapeDtypeStruct((B, G_A * VROWS_A, S), bf),
            jax.ShapeDtypeStruct((B, G_A, rep, S), f32),
            jax.ShapeDtypeStruct((B, G_A, 1, S), f32),
            jax.ShapeDtypeStruct((B, H_B * DQ_PAD, S), bf),
            jax.ShapeDtypeStruct((B, H_B, S, DQ_PAD), bf),
            jax.ShapeDtypeStruct((B, H_B * VROWS_B, S), bf),
            jax.ShapeDtypeStruct((B, H_B // 2, 2, S), f32),
            jax.ShapeDtypeStruct((B, H_B // 2, 2, S), f32),
        ],
        compiler_params=_cparams(("parallel", "parallel")),
        name="proj",
    )(x, W["ln1"], W["w1t"], W["tabs"], W["gcq"], W["gckv"], W["wuqt"], W["wukvt"])


def _attn_kernel(qt_ref, k_ref, vt_ref, qn_ref, kn_ref, ot_ref, s_ref, p_ref, *, heads, rep, dv, tk, heads_per_loop):
    S = k_ref.shape[2]
    tq = qt_ref.shape[2]
    nk = S // tk
    vrows = dv + ONES_ROWS

    def scores(h, c):
        off = pl.multiple_of(c * tk, tk)
        kc = k_ref[0, h // rep, pl.ds(off, tk), :]
        return _dot(kc, qt_ref[0, h * DQ_PAD:(h + 1) * DQ_PAD, :])

    def vchunk(h, c):
        off = pl.multiple_of(c * tk, tk)
        g = h // rep
        return vt_ref[0, g * vrows:(g + 1) * vrows, pl.ds(off, tk)]

    def finish(h, acc):
        ot_ref[0, h * dv:(h + 1) * dv, :] = (acc[:dv] / acc[dv:dv + 1]).astype(ot_ref.dtype)

    zero_acc = lambda: jnp.zeros((vrows, tq), jnp.float32)
    bound2 = jnp.max(qn_ref[0, 0]) * jnp.max(kn_ref[0, 0])
    safe = bound2 <= SAFE_SCORE_BOUND * SAFE_SCORE_BOUND

    def probs(s, m, with_max):
        if not with_max:
            return m, None, jnp.exp2(s).astype(MM_DTYPE)
        mn = jnp.maximum(m, jnp.max(s, axis=0, keepdims=True))
        return mn, jnp.exp2(m - mn), jnp.exp2(s - mn).astype(MM_DTYPE)

    def accumulate(h, c, p_slot, alpha, acc):
        pv = _dot(vchunk(h, c), p_slot[...])
        return pv + (acc if alpha is None else alpha * acc)

    def run(with_max):
        for h0 in range(0, heads, heads_per_loop):
            hs = list(range(h0, h0 + heads_per_loop))
            for i, h in enumerate(hs):
                s_ref[0, i] = scores(h, 0)
                p_ref[1, i] = jnp.zeros((tk, tq), MM_DTYPE)

            def body(j, carry, hs=hs):
                c0 = 2 * j
                prev = jnp.maximum(c0 - 1, 0)
                nxt = jnp.minimum(c0 + 2, nk - 1)
                out = []
                for i, (h, (m, alpha_b, acc)) in enumerate(zip(hs, carry)):
                    s_ref[1, i] = scores(h, c0 + 1)
                    acc = accumulate(h, prev, p_ref.at[1, i], alpha_b, acc)
                    m, alpha_a, pa = probs(s_ref[0, i], m, with_max)
                    p_ref[0, i] = pa
                    acc = accumulate(h, c0, p_ref.at[0, i], alpha_a, acc)
                    s_ref[0, i] = scores(h, nxt)
                    m, alpha_b, pb = probs(s_ref[1, i], m, with_max)
                    p_ref[1, i] = pb
                    out.append((m, alpha_b, acc))
                return tuple(out)

            one = jnp.ones((1, tq), jnp.float32) if with_max else None
            init = tuple((jnp.full((1, tq), -jnp.inf, jnp.float32), one, zero_acc()) for _ in hs)
            res = lax.fori_loop(0, nk // 2, body, init, unroll=ATTN_UNROLL)
            for i, ((m, alpha_b, acc), h) in enumerate(zip(res, hs)):
                finish(h, accumulate(h, nk - 1, p_ref.at[1, i], alpha_b, acc))

    pl.when(safe)(lambda: run(False))
    pl.when(jnp.logical_not(safe))(lambda: run(True))


def _attention(qt, k, vt, qn, kn, *, groups_per_step, rep, dv, name):
    B, _, S = qt.shape
    G = k.shape[1]
    gs = groups_per_step
    heads = gs * rep
    tq = TQ_ATTN
    hpl = min(2, heads)
    vrows = dv + ONES_ROWS
    assert (S // TK_ATTN) % 2 == 0
    kern = functools.partial(_attn_kernel, heads=heads, rep=rep, dv=dv, tk=TK_ATTN, heads_per_loop=hpl)
    return pl.pallas_call(
        kern,
        grid=(B, G // gs, S // tq),
        in_specs=[
            pl.BlockSpec((1, heads * DQ_PAD, tq), lambda b, g, i: (b, g, i)),
            pl.BlockSpec((1, gs, S, DQ_PAD), lambda b, g, i: (b, g, 0, 0)),
            pl.BlockSpec((1, gs * vrows, S), lambda b, g, i: (b, g, 0)),
            pl.BlockSpec((1, 1, heads, tq), lambda b, g, i: (b, g, 0, i)),
            pl.BlockSpec((1, 1, gs, S), lambda b, g, i: (b, g, 0, 0)),
        ],
        out_specs=pl.BlockSpec((1, heads * dv, tq), lambda b, g, i: (b, g, i)),
        out_shape=jax.ShapeDtypeStruct((B, G * rep * dv, S), MM_DTYPE),
        scratch_shapes=[pltpu.VMEM((2, hpl, TK_ATTN, tq), jnp.float32),
                        pltpu.VMEM((2, hpl, TK_ATTN, tq), MM_DTYPE)],
        compiler_params=_cparams(("parallel", "parallel", "arbitrary")),
        name=name,
    )(qt, k, vt, qn, kn)


def _post_kernel(x_ref, ln1_ref, wgt_ref, oat_ref, obt_ref, wbrat_ref, wbrbt_ref, woutt_ref, g2_ref,
                 ht_ref, xn2t_ref):
    bf = MM_DTYPE
    x = x_ref[0]
    xn = x * lax.rsqrt(jnp.mean(x * x, axis=-1, keepdims=True) + EPS) * ln1_ref[...]
    xnt = xn.T.astype(bf)
    gates = jax.nn.sigmoid(_dot(wgt_ref[...], xnt))
    ma = _dot(wbrat_ref[...], oat_ref[0])
    mb = _dot(wbrbt_ref[...], obt_ref[0])
    merged = gates[:D_MODEL] * ma + gates[D_MODEL:] * mb
    ht = x.T + _dot(woutt_ref[...], merged.astype(bf))
    ht_ref[0] = ht
    r2 = lax.rsqrt(jnp.mean(ht * ht, axis=0, keepdims=True) + EPS)
    xn2t_ref[0] = (ht * r2 * g2_ref[...]).astype(bf)


def _post(x, oat, obt, W):
    B, S, _ = x.shape
    tm = TM_POST
    const = lambda shape: pl.BlockSpec(shape, lambda b, i: (0,) * len(shape))
    return pl.pallas_call(
        _post_kernel,
        grid=(B, S // tm),
        in_specs=[
            pl.BlockSpec((1, tm, D_MODEL), lambda b, i: (b, i, 0)),
            const((1, D_MODEL)), const((2 * D_MODEL, D_MODEL)),
            pl.BlockSpec((1, H_A * HD_A, tm), lambda b, i: (b, 0, i)),
            pl.BlockSpec((1, H_B * V_B, tm), lambda b, i: (b, 0, i)),
            const((D_MODEL, H_A * HD_A)), const((D_MODEL, H_B * V_B)), const((D_MODEL, D_MODEL)),
            const((D_MODEL, tm)),
        ],
        out_specs=[
            pl.BlockSpec((1, D_MODEL, tm), lambda b, i: (b, 0, i)),
            pl.BlockSpec((1, D_MODEL, tm), lambda b, i: (b, 0, i)),
        ],
        out_shape=[
            jax.ShapeDtypeStruct((B, D_MODEL, S), jnp.float32),
            jax.ShapeDtypeStruct((B, D_MODEL, S), MM_DTYPE),
        ],
        compiler_params=_cparams(("parallel", "parallel")),
        name="post",
    )(x, W["ln1"], W["wgt"], oat, obt, W["wbrat"], W["wbrbt"], W["woutt"], W["g2"])


def _cmpx(xs, i, j):
    a, b = xs[i], xs[j]
    xs[i], xs[j] = jnp.maximum(a, b), jnp.minimum(a, b)


def _bitonic_merge_desc(xs):
    n = len(xs)
    j = n // 2
    while j >= 1:
        for i in range(n):
            l = i ^ j
            if l > i:
                _cmpx(xs, i, l)
        j //= 2


def _bitonic_sort_desc(xs):
    n = len(xs)
    k = 2
    while k <= n:
        j = k // 2
        while j >= 1:
            for i in range(n):
                l = i ^ j
                if l > i:
                    if (i & k) == 0:
                        _cmpx(xs, i, l)
                    else:
                        _cmpx(xs, l, i)
            j //= 2
        k *= 2


_CAND_PAIRS = [(i, j) for i in range(PEER_TOPK) for j in range(PEER_TOPK) if (i + 1) * (j + 1) <= PEER_TOPK]


def _route_kernel(xn2t_ref, wpqt_ref, subk_ref, s_ref, stats_ref, top1_ref):
    bf = MM_DTYPE
    tr = xn2t_ref.shape[2]
    groups = tr // LANES
    qt = _dot(wpqt_ref[...], xn2t_ref[0]).astype(bf)
    row = lax.broadcasted_iota(jnp.int32, (SUBLANES, LANES), 0)
    nblk = N_KEYS // SUBLANES
    for h in range(PEER_HEADS):
        tops = []
        for pp in range(2):
            hp = h * 2 + pp
            st = _dot(subk_ref[hp], qt[hp * HALF_KEY:(hp + 1) * HALF_KEY])
            s_ref[0, hp * N_KEYS:(hp + 1) * N_KEYS, :] = st
            xs = [st[v * SUBLANES:(v + 1) * SUBLANES] for v in range(nblk)]
            _bitonic_sort_desc(xs)
            for shift in (4, 2, 1):
                rolled = [pltpu.roll(xs[nblk - 1 - v], shift, 0) for v in range(nblk)]
                xs = [jnp.maximum(a, b) for a, b in zip(xs, rolled)]
                _bitonic_merge_desc(xs)
            comp = []
            for v in range(PEER_TOPK):
                c = xs[v][:, 0:LANES]
                for gidx in range(1, groups):
                    c = jnp.where(row == gidx, xs[v][:, gidx * LANES:(gidx + 1) * LANES], c)
                comp.append(c)
            tops.append(comp)
        a, b = tops
        cands = [a[i] + b[j] for (i, j) in _CAND_PAIRS]
        cands += [jnp.full((SUBLANES, LANES), -jnp.inf, jnp.float32)] * (64 - len(cands))
        _bitonic_sort_desc(cands)
        cmax = cands[0]
        z = jnp.zeros((SUBLANES, LANES), jnp.float32)
        for kk in range(PEER_TOPK):
            z = z + jnp.exp(cands[kk] - cmax)
        def wide(val):
            return jnp.concatenate([val[gidx:gidx + 1, :] for gidx in range(groups)], axis=1)

        for r, val in enumerate((cands[PEER_TOPK - 1], a[0], b[0], z)):
            stats_ref[0, h * 4 + r:h * 4 + r + 1, :] = wide(val)
        for v in range(PEER_TOPK):
            top1_ref[0, h * PEER_TOPK + v:h * PEER_TOPK + v + 1, :] = wide(a[v])


def _route(xn2t, W):
    B, _, S = xn2t.shape
    tr = TR_ROUTE
    const = lambda shape: pl.BlockSpec(shape, lambda b, i: (0,) * len(shape))
    return pl.pallas_call(
        _route_kernel,
        grid=(B, S // tr),
        in_specs=[
            pl.BlockSpec((1, D_MODEL, tr), lambda b, i: (b, 0, i)),
            const((PEER_HEADS * D_KEY, D_MODEL)),
            const((PEER_HEADS * 2, N_KEYS, HALF_KEY)),
        ],
        out_specs=[
            pl.BlockSpec((1, PEER_HEADS * 2 * N_KEYS, tr), lambda b, i: (b, 0, i)),
            pl.BlockSpec((1, PEER_HEADS * 4, tr), lambda b, i: (b, 0, i)),
            pl.BlockSpec((1, PEER_HEADS * PEER_TOPK, tr), lambda b, i: (b, 0, i)),
        ],
        out_shape=[
            jax.ShapeDtypeStruct((B, PEER_HEADS * 2 * N_KEYS, S), jnp.float32),
            jax.ShapeDtypeStruct((B, PEER_HEADS * 4, S), jnp.float32),
            jax.ShapeDtypeStruct((B, PEER_HEADS * PEER_TOPK, S), jnp.float32),
        ],
        compiler_params=_cparams(("parallel", "parallel")),
        name="route",
    )(xn2t, W["wpqt"], W["subk"])


def _peer_kernel(xn2t_ref, s_ref, stats_ref, top1_ref, ht_ref, u_ref, vt_ref, gfin_ref, y_ref,
                 e1_ref, e2_ref, acc_ref, hid_ref, a_ref, s1_ref, thr_ref):
    e = pl.program_id(2)
    ne = pl.num_programs(2)
    ec = u_ref.shape[0]
    tt = xn2t_ref.shape[2]
    rows_per_chunk = ec // N_KEYS

    @pl.when(e == 0)
    def _():
        for h in range(PEER_HEADS):
            amax = stats_ref[0, h * 4 + 1:h * 4 + 2, :]
            bmax = stats_ref[0, h * 4 + 2:h * 4 + 3, :]
            z = stats_ref[0, h * 4 + 3:h * 4 + 4, :]
            s1 = s_ref[0, (2 * h) * N_KEYS:(2 * h + 1) * N_KEYS, :]
            s2 = s_ref[0, (2 * h + 1) * N_KEYS:(2 * h + 2) * N_KEYS, :]
            e1_ref[h * N_KEYS:(h + 1) * N_KEYS, :] = jnp.exp(s1 - amax)
            e2_ref[h * N_KEYS:(h + 1) * N_KEYS, :] = jnp.exp(s2 - bmax) / z
            s1_ref[h * N_KEYS:(h + 1) * N_KEYS, :] = s1
            tau = stats_ref[0, h * 4:h * 4 + 1, :]
            thr = jnp.full((N_KEYS, tt), jnp.inf, jnp.float32)
            for r in range(PEER_TOPK):
                ar = top1_ref[0, h * PEER_TOPK + r:h * PEER_TOPK + r + 1, :]
                thr = jnp.where(ar + s2 >= tau, ar, thr)
            thr_ref[h * N_KEYS:(h + 1) * N_KEYS, :] = thr
        acc_ref[...] = jnp.zeros_like(acc_ref)

    def build(rs):
        first = [e * rows_per_chunk + r for r in rs]
        s1rows = [[s1_ref[pl.ds(h * N_KEYS + i, 1), :] for h in range(PEER_HEADS)] for i in first]
        e1rows = [[e1_ref[pl.ds(h * N_KEYS + i, 1), :] for h in range(PEER_HEADS)] for i in first]
        for t0 in range(0, tt, LANES):
            lanes = slice(t0, t0 + LANES)
            for j0 in range(0, N_KEYS, PEER_TJ):
                ws = [jnp.zeros((PEER_TJ, LANES), jnp.float32) for _ in rs]
                for h in range(PEER_HEADS):
                    thr = thr_ref[h * N_KEYS + j0:h * N_KEYS + j0 + PEER_TJ, lanes]
                    e2 = e2_ref[h * N_KEYS + j0:h * N_KEYS + j0 + PEER_TJ, lanes]
                    for k in range(len(rs)):
                        ws[k] = ws[k] + jnp.where(s1rows[k][h][:, lanes] >= thr,
                                                  e1rows[k][h][:, lanes] * e2, 0.0)
                for k, r in enumerate(rs):
                    rows = slice(r * N_KEYS + j0, r * N_KEYS + j0 + PEER_TJ)
                    hid = hid_ref[rows, lanes]
                    act = 0.5 * hid * (1.0 + lax.erf(hid * (2.0 ** -0.5)))
                    a_ref[rows, lanes] = (act * ws[k]).astype(MM_DTYPE)

    part = ec // PEER_PARTS
    rows_part = rows_per_chunk // PEER_PARTS
    xt = xn2t_ref[0]
    for k in range(PEER_PARTS):
        hid_ref[k * part:(k + 1) * part] = _dot(u_ref[k * part:(k + 1) * part], xt)
    for k in range(PEER_PARTS):
        build(list(range(k * rows_part, (k + 1) * rows_part)))
        acc_ref[...] += _dot(vt_ref[:, k * part:(k + 1) * part], a_ref[k * part:(k + 1) * part])

    @pl.when(e == ne - 1)
    def _():
        yt = ht_ref[0] + acc_ref[...]
        r = lax.rsqrt(jnp.mean(yt * yt, axis=0, keepdims=True) + EPS)
        y_ref[0] = (yt * r * gfin_ref[...]).T


def _peer(xn2t, s, stats, top1, ht, W):
    B, _, S = xn2t.shape
    tt, ec = TT_PEER, EC_PEER
    return pl.pallas_call(
        _peer_kernel,
        grid=(B, S // tt, N_EXPERTS // ec),
        in_specs=[
            pl.BlockSpec((1, D_MODEL, tt), lambda b, i, e: (b, 0, i)),
            pl.BlockSpec((1, PEER_HEADS * 2 * N_KEYS, tt), lambda b, i, e: (b, 0, i)),
            pl.BlockSpec((1, PEER_HEADS * 4, tt), lambda b, i, e: (b, 0, i)),
            pl.BlockSpec((1, PEER_HEADS * PEER_TOPK, tt), lambda b, i, e: (b, 0, i)),
            pl.BlockSpec((1, D_MODEL, tt), lambda b, i, e: (b, 0, i)),
            pl.BlockSpec((ec, D_MODEL), lambda b, i, e: (e, 0)),
            pl.BlockSpec((D_MODEL, ec), lambda b, i, e: (0, e)),
            pl.BlockSpec((D_MODEL, tt), lambda b, i, e: (0, 0)),
        ],
        out_specs=pl.BlockSpec((1, tt, D_MODEL), lambda b, i, e: (b, i, 0)),
        out_shape=jax.ShapeDtypeStruct((B, S, D_MODEL), jnp.float32),
        scratch_shapes=[
            pltpu.VMEM((PEER_HEADS * N_KEYS, tt), jnp.float32),
            pltpu.VMEM((PEER_HEADS * N_KEYS, tt), jnp.float32),
            pltpu.VMEM((D_MODEL, tt), jnp.float32),
            pltpu.VMEM((ec, tt), jnp.float32),
            pltpu.VMEM((ec, tt), MM_DTYPE),
            pltpu.VMEM((PEER_HEADS * N_KEYS, tt), jnp.float32),
            pltpu.VMEM((PEER_HEADS * N_KEYS, tt), jnp.float32),
        ],
        compiler_params=_cparams(("parallel", "parallel", "arbitrary")),
        name="peer",
    )(xn2t, s, stats, top1, ht, W["u"], W["vt"], W["gfin"])


def _forward(x, W):
    qat, ka, vat, qan, kan, qbt, kb, vbt, qbn, kbn = _proj(x, W)
    oat = _attention(qat, ka, vat, qan, kan, groups_per_step=1, rep=H_A // G_A, dv=HD_A, name="attn_gqa")
    obt = _attention(qbt, kb, vbt, qbn, kbn, groups_per_step=2, rep=1, dv=V_B, name="attn_mla")
    ht, xn2t = _post(x, oat, obt, W)
    s, stats, top1 = _route(xn2t, W)
    return _peer(xn2t, s, stats, top1, ht, W)


def kernel(x_prompt, x_sample, ln1_g, w_in, qa_norm_g, ka_norm_g, cq_norm_g, ckv_norm_g, w_uq, w_ukv,
           w_br_a, w_br_b, w_out, ln2_g, w_pq, sub_keys, expert_u, expert_v, final_g):
    params = (ln1_g, w_in, qa_norm_g, ka_norm_g, cq_norm_g, ckv_norm_g, w_uq, w_ukv,
              w_br_a, w_br_b, w_out, ln2_g, w_pq, sub_keys, expert_u, expert_v, final_g)
    assert x_prompt.shape[1] == x_sample.shape[1]
    W = _prepare(x_prompt.shape[1], *params)
    return (_forward(x_prompt, W), _forward(x_sample, W))
```

```python
import functools
import math

import numpy as np
import jax
import jax.numpy as jnp
from jax import lax
from jax.experimental import pallas as pl
from jax.experimental.pallas import tpu as pltpu

D_MODEL = 1024
GRID_W = 64
ROPE_THETA = 10000.0
EPS = 1e-6
H_A, G_A, HD_A = 8, 2, 64
H_B, Q_LORA, KV_LORA, NOPE_B, ROPE_B, V_B = 8, 256, 128, 64, 32, 64
PEER_HEADS, N_KEYS, D_KEY, PEER_TOPK = 8, 128, 256, 16
N_EXPERTS = N_KEYS * N_KEYS
HALF_KEY = D_KEY // 2

LANES = 128
SUBLANES = 8
VMEM_LIMIT_BYTES = 56 * 1024 * 1024

DQ_PAD = LANES
LOG2E = math.log2(math.e)
MM_DTYPE = jnp.bfloat16
ONES_ROWS = 2 * SUBLANES
VROWS_A = HD_A + ONES_ROWS
VROWS_B = V_B + ONES_ROWS
SAFE_SCORE_BOUND = 60.0

TM_PROJ = 512
TQ_ATTN = 512
TK_ATTN = 256
ATTN_UNROLL = 16
TM_POST = 512
TR_ROUTE = SUBLANES * LANES
TT_PEER = 512
EC_PEER = 2048
PEER_PARTS = 4
PEER_TJ = 32

_R_QA, _R_QA2, _R_KA, _R_KA2, _R_VA, _R_CQ, _R_CKV, _R_KR, _R_KR2, _R_END = (
    0, 512, 1024, 1152, 1280, 1408, 1664, 1792, 1824, 1856)
_Q2_ROPE = H_B * NOPE_B
_Q2_ROPE2 = _Q2_ROPE + H_B * ROPE_B
_Q2_END = _Q2_ROPE2 + H_B * ROPE_B
_T_QAC, _T_QAS, _T_KAC, _T_KAS, _T_QBC, _T_QBS, _T_KBC, _T_KBS, _T_END = (
    0, 64, 128, 192, 256, 288, 320, 352, 384)


def _cparams(sem):
    return pltpu.CompilerParams(dimension_semantics=sem, vmem_limit_bytes=VMEM_LIMIT_BYTES)


def _dot(a, b):
    return jnp.dot(a, b, preferred_element_type=jnp.float32)


def _rope_partner(d):
    half, qtr = d // 2, d // 4
    e = np.arange(d)
    within = e % half
    first = within < qtr
    partner = np.where(first, e + qtr, e - qtr)
    sign = np.where(first, -1.0, 1.0).astype(np.float32)
    freq_idx = within % qtr
    use_col = e >= half
    return partner, sign, freq_idx, use_col


def _rope_tables(S, d):
    half = d // 2
    partner, sign, freq_idx, use_col = _rope_partner(d)
    t = jnp.arange(S, dtype=jnp.int32)
    r = (t // GRID_W).astype(jnp.float32)
    c = (t % GRID_W).astype(jnp.float32)
    freqs = ROPE_THETA ** (-jnp.arange(0, half, 2, dtype=jnp.float32) / half)
    f = freqs[freq_idx]
    pos = jnp.where(jnp.asarray(use_col)[:, None], c[None, :], r[None, :])
    ang = pos * f[:, None]
    return jnp.cos(ang), jnp.sin(ang) * jnp.asarray(sign)[:, None], partner


def _prepare(S, ln1_g, w_in, qa_norm_g, ka_norm_g, cq_norm_g, ckv_norm_g, w_uq, w_ukv,
             w_br_a, w_br_b, w_out, ln2_g, w_pq, sub_keys, expert_u, expert_v, final_g):
    bf = MM_DTYPE
    w = w_in[0]
    c_qa, c_ka, c_va, c_cq, c_ckv, c_kr, c_ga = np.cumsum(
        [H_A * HD_A, G_A * HD_A, G_A * HD_A, Q_LORA, KV_LORA, ROPE_B, D_MODEL]).tolist()
    p64, _, _, _ = _rope_partner(HD_A)
    p32, _, _, _ = _rope_partner(ROPE_B)
    qa_perm = (np.arange(H_A)[:, None] * HD_A + p64[None, :]).reshape(-1)
    ka_perm = (np.arange(G_A)[:, None] * HD_A + p64[None, :]).reshape(-1)
    w_qa, w_ka = w[:, :c_qa], w[:, c_qa:c_ka]
    w_kr = w[:, c_ckv:c_kr]
    w1 = jnp.concatenate([
        w_qa, w_qa[:, qa_perm], w_ka, w_ka[:, ka_perm], w[:, c_ka:c_va],
        w[:, c_va:c_cq], w[:, c_cq:c_ckv], w_kr, w_kr[:, p32]], axis=1)
    w1t = w1.T.astype(bf)
    wgt = w[:, c_kr:].T.astype(bf)

    uq = w_uq[0].reshape(Q_LORA, H_B, NOPE_B + ROPE_B)
    uq_nope = uq[:, :, :NOPE_B].reshape(Q_LORA, H_B * NOPE_B)
    uq_rope = uq[:, :, NOPE_B:]
    wuqt = jnp.concatenate([uq_nope, uq_rope.reshape(Q_LORA, -1),
                            uq_rope[:, :, p32].reshape(Q_LORA, -1)], axis=1).T.astype(bf)
    ukv = w_ukv[0].reshape(KV_LORA, H_B, NOPE_B + V_B)
    wukvt = jnp.concatenate([ukv[:, :, :NOPE_B].reshape(KV_LORA, -1),
                             ukv[:, :, NOPE_B:].reshape(KV_LORA, -1)], axis=1).T.astype(bf)

    cos64, sin64, _ = _rope_tables(S, HD_A)
    cos32, sin32, _ = _rope_tables(S, ROPE_B)
    sa = (HD_A ** -0.5) * LOG2E
    sb = ((NOPE_B + ROPE_B) ** -0.5) * LOG2E
    gq, gk = qa_norm_g[0], ka_norm_g[0]
    tabs = jnp.concatenate([
        cos64 * (gq * sa)[:, None], sin64 * (gq[p64] * sa)[:, None],
        cos64 * gk[:, None], sin64 * gk[p64][:, None],
        cos32 * sb, sin32 * sb, cos32, sin32], axis=0)

    def bcast(g, n):
        return jnp.broadcast_to(g.reshape(-1, 1), (g.size, n))

    return dict(
        ln1=ln1_g[0].reshape(1, D_MODEL), w1t=w1t, wgt=wgt, wuqt=wuqt, wukvt=wukvt, tabs=tabs,
        gcq=bcast(cq_norm_g[0], TM_PROJ), gckv=bcast(ckv_norm_g[0], TM_PROJ),
        wbrat=w_br_a[0].T.astype(bf), wbrbt=w_br_b[0].T.astype(bf), woutt=w_out[0].T.astype(bf),
        g2=bcast(ln2_g[0], TM_POST), wpqt=w_pq[0].T.astype(bf),
        subk=sub_keys[0].reshape(PEER_HEADS * 2, N_KEYS, HALF_KEY).astype(bf),
        u=expert_u[0].astype(bf), vt=expert_v[0].T.astype(bf),
        gfin=bcast(final_g, TT_PEER), sb=sb)


def _proj_kernel(x_ref, ln1_ref, w1t_ref, tab_ref, gcq_ref, gckv_ref, wuqt_ref, wukvt_ref,
                 qat_ref, ka_ref, vat_ref, qan_ref, kan_ref, qbt_ref, kb_ref, vbt_ref, qbn_ref, kbn_ref, *, sb):
    bf = MM_DTYPE
    x = x_ref[0]
    tm = x.shape[0]
    xn = x * lax.rsqrt(jnp.mean(x * x, axis=-1, keepdims=True) + EPS) * ln1_ref[...]
    xnt = xn.T.astype(bf)
    p = _dot(w1t_ref[...], xnt)
    tab = tab_ref[...]
    zeros64 = jnp.zeros((64, tm), jnp.float32)
    zeros32 = jnp.zeros((32, tm), jnp.float32)
    ones = jnp.ones((ONES_ROWS, tm), bf)

    def rstd(y):
        return lax.rsqrt(jnp.mean(y * y, axis=0, keepdims=True) + EPS)

    def sqnorm(yb):
        yf = yb.astype(jnp.float32)
        return jnp.sum(yf * yf, axis=0, keepdims=True)

    rep = H_A // G_A
    for h in range(H_A):
        y = p[_R_QA + h * HD_A:_R_QA + (h + 1) * HD_A]
        y2 = p[_R_QA2 + h * HD_A:_R_QA2 + (h + 1) * HD_A]
        q = ((y * tab[_T_QAC:_T_QAS] + y2 * tab[_T_QAS:_T_KAC]) * rstd(y)).astype(bf)
        qat_ref[0, h * DQ_PAD:h * DQ_PAD + HD_A, :] = q
        qat_ref[0, h * DQ_PAD + HD_A:(h + 1) * DQ_PAD, :] = zeros64.astype(bf)
        qan_ref[0, h // rep, h % rep:h % rep + 1, :] = sqnorm(q)
    for g in range(G_A):
        y = p[_R_KA + g * HD_A:_R_KA + (g + 1) * HD_A]
        y2 = p[_R_KA2 + g * HD_A:_R_KA2 + (g + 1) * HD_A]
        k = (y * tab[_T_KAC:_T_KAS] + y2 * tab[_T_KAS:_T_QBC]) * rstd(y)
        ka_ref[0, g] = jnp.concatenate([k, zeros64], axis=0).T.astype(bf)
        kan_ref[0, g, 0:1, :] = sqnorm(k.astype(bf))
        vat_ref[0, g * VROWS_A:g * VROWS_A + HD_A, :] = p[_R_VA + g * HD_A:_R_VA + (g + 1) * HD_A].astype(bf)
        vat_ref[0, g * VROWS_A + HD_A:(g + 1) * VROWS_A, :] = ones

    cq = p[_R_CQ:_R_CKV]
    cqn = (cq * rstd(cq) * gcq_ref[...]).astype(bf)
    q2 = _dot(wuqt_ref[...], cqn)
    for h in range(H_B):
        nope = (q2[h * NOPE_B:(h + 1) * NOPE_B] * sb).astype(bf)
        yr = q2[_Q2_ROPE + h * ROPE_B:_Q2_ROPE + (h + 1) * ROPE_B]
        yr2 = q2[_Q2_ROPE2 + h * ROPE_B:_Q2_ROPE2 + (h + 1) * ROPE_B]
        rope = (yr * tab[_T_QBC:_T_QBS] + yr2 * tab[_T_QBS:_T_KBC]).astype(bf)
        base = h * DQ_PAD
        qbt_ref[0, base:base + NOPE_B, :] = nope
        qbt_ref[0, base + NOPE_B:base + NOPE_B + ROPE_B, :] = rope
        qbt_ref[0, base + NOPE_B + ROPE_B:base + DQ_PAD, :] = zeros32.astype(bf)
        qbn_ref[0, h // 2, h % 2:h % 2 + 1, :] = sqnorm(nope) + sqnorm(rope)
    ckv = p[_R_CKV:_R_KR]
    ckvn = (ckv * rstd(ckv) * gckv_ref[...]).astype(bf)
    kv = _dot(wukvt_ref[...], ckvn)
    krope = p[_R_KR:_R_KR2] * tab[_T_KBC:_T_KBS] + p[_R_KR2:_R_END] * tab[_T_KBS:_T_END]
    krope_n = sqnorm(krope.astype(bf))
    for h in range(H_B):
        knope = kv[h * NOPE_B:(h + 1) * NOPE_B]
        kt = jnp.concatenate([knope, krope, zeros32], axis=0)
        kb_ref[0, h] = kt.T.astype(bf)
        kbn_ref[0, h // 2, h % 2:h % 2 + 1, :] = sqnorm(knope.astype(bf)) + krope_n
        v0 = H_B * NOPE_B + h * V_B
        vbt_ref[0, h * VROWS_B:h * VROWS_B + V_B, :] = kv[v0:v0 + V_B].astype(bf)
        vbt_ref[0, h * VROWS_B + V_B:(h + 1) * VROWS_B, :] = ones


def _proj(x, W):
    B, S, _ = x.shape
    tm = TM_PROJ
    bf = MM_DTYPE
    f32 = jnp.float32
    rep = H_A // G_A
    const = lambda shape: pl.BlockSpec(shape, lambda b, i: (0,) * len(shape))
    return pl.pallas_call(
        functools.partial(_proj_kernel, sb=W["sb"]),
        grid=(B, S // tm),
        in_specs=[
            pl.BlockSpec((1, tm, D_MODEL), lambda b, i: (b, i, 0)),
            const((1, D_MODEL)), const((_R_END, D_MODEL)),
            pl.BlockSpec((_T_END, tm), lambda b, i: (0, i)),
            const((Q_LORA, tm)), const((KV_LORA, tm)),
            const((_Q2_END, Q_LORA)), const((H_B * (NOPE_B + V_B), KV_LORA)),
        ],
        out_specs=[
            pl.BlockSpec((1, H_A * DQ_PAD, tm), lambda b, i: (b, 0, i)),
            pl.BlockSpec((1, G_A, tm, DQ_PAD), lambda b, i: (b, 0, i, 0)),
            pl.BlockSpec((1, G_A * VROWS_A, tm), lambda b, i: (b, 0, i)),
            pl.BlockSpec((1, G_A, rep, tm), lambda b, i: (b, 0, 0, i)),
            pl.BlockSpec((1, G_A, 1, tm), lambda b, i: (b, 0, 0, i)),
            pl.BlockSpec((1, H_B * DQ_PAD, tm), lambda b, i: (b, 0, i)),
            pl.BlockSpec((1, H_B, tm, DQ_PAD), lambda b, i: (b, 0, i, 0)),
            pl.BlockSpec((1, H_B * VROWS_B, tm), lambda b, i: (b, 0, i)),
            pl.BlockSpec((1, H_B // 2, 2, tm), lambda b, i: (b, 0, 0, i)),
            pl.BlockSpec((1, H_B // 2, 2, tm), lambda b, i: (b, 0, 0, i)),
        ],
        out_shape=[
            jax.ShapeDtypeStruct((B, H_A * DQ_PAD, S), bf),
            jax.ShapeDtypeStruct((B, G_A, S, DQ_PAD), bf),
            jax.ShapeDtypeStruct((B, G_A * VROWS_A, S), bf),
            jax.ShapeDtypeStruct((B, G_A, rep, S), f32),
            jax.ShapeDtypeStruct((B, G_A, 1, S), f32),
            jax.ShapeDtypeStruct((B, H_B * DQ_PAD, S), bf),
            jax.ShapeDtypeStruct((B, H_B, S, DQ_PAD), bf),
            jax.ShapeDtypeStruct((B, H_B * VROWS_B, S), bf),
            jax.ShapeDtypeStruct((B, H_B // 2, 2, S), f32),
            jax.ShapeDtypeStruct((B, H_B // 2, 2, S), f32),
        ],
        compiler_params=_cparams(("parallel", "parallel")),
        name="proj",
    )(x, W["ln1"], W["w1t"], W["tabs"], W["gcq"], W["gckv"], W["wuqt"], W["wukvt"])


def _attn_kernel(qt_ref, k_ref, vt_ref, qn_ref, kn_ref, ot_ref, s_ref, p_ref, *, heads, rep, dv, tk, heads_per_loop):
    S = k_ref.shape[2]
    tq = qt_ref.shape[2]
    nk = S // tk
    vrows = dv + ONES_ROWS

    def scores(h, c):
        off = pl.multiple_of(c * tk, tk)
        kc = k_ref[0, h // rep, pl.ds(off, tk), :]
        return _dot(kc, qt_ref[0, h * DQ_PAD:(h + 1) * DQ_PAD, :])

    def vchunk(h, c):
        off = pl.multiple_of(c * tk, tk)
        g = h // rep
        return vt_ref[0, g * vrows:(g + 1) * vrows, pl.ds(off, tk)]

    def finish(h, acc):
        ot_ref[0, h * dv:(h + 1) * dv, :] = (acc[:dv] / acc[dv:dv + 1]).astype(ot_ref.dtype)

    zero_acc = lambda: jnp.zeros((vrows, tq), jnp.float32)
    bound2 = jnp.max(qn_ref[0, 0]) * jnp.max(kn_ref[0, 0])
    safe = bound2 <= SAFE_SCORE_BOUND * SAFE_SCORE_BOUND

    def probs(s, m, with_max):
        if not with_max:
            return m, None, jnp.exp2(s).astype(MM_DTYPE)
        mn = jnp.maximum(m, jnp.max(s, axis=0, keepdims=True))
        return mn, jnp.exp2(m - mn), jnp.exp2(s - mn).astype(MM_DTYPE)

    def accumulate(h, c, p_slot, alpha, acc):
        pv = _dot(vchunk(h, c), p_slot[...])
        return pv + (acc if alpha is None else alpha * acc)

    def run(with_max):
        for h0 in range(0, heads, heads_per_loop):
            hs = list(range(h0, h0 + heads_per_loop))
            for i, h in enumerate(hs):
                s_ref[0, i] = scores(h, 0)
                p_ref[1, i] = jnp.zeros((tk, tq), MM_DTYPE)

            def body(j, carry, hs=hs):
                c0 = 2 * j
                prev = jnp.maximum(c0 - 1, 0)
                nxt = jnp.minimum(c0 + 2, nk - 1)
                out = []
                for i, (h, (m, alpha_b, acc)) in enumerate(zip(hs, carry)):
                    s_ref[1, i] = scores(h, c0 + 1)
                    acc = accumulate(h, prev, p_ref.at[1, i], alpha_b, acc)
                    m, alpha_a, pa = probs(s_ref[0, i], m, with_max)
                    p_ref[0, i] = pa
                    acc = accumulate(h, c0, p_ref.at[0, i], alpha_a, acc)
                    s_ref[0, i] = scores(h, nxt)
                    m, alpha_b, pb = probs(s_ref[1, i], m, with_max)
                    p_ref[1, i] = pb
                    out.append((m, alpha_b, acc))
                return tuple(out)

            one = jnp.ones((1, tq), jnp.float32) if with_max else None
            init = tuple((jnp.full((1, tq), -jnp.inf, jnp.float32), one, zero_acc()) for _ in hs)
            res = lax.fori_loop(0, nk // 2, body, init, unroll=ATTN_UNROLL)
            for i, ((m, alpha_b, acc), h) in enumerate(zip(res, hs)):
                finish(h, accumulate(h, nk - 1, p_ref.at[1, i], alpha_b, acc))

    pl.when(safe)(lambda: run(False))
    pl.when(jnp.logical_not(safe))(lambda: run(True))


def _attention(qt, k, vt, qn, kn, *, groups_per_step, rep, dv, name):
    B, _, S = qt.shape
    G = k.shape[1]
    gs = groups_per_step
    heads = gs * rep
    tq = TQ_ATTN
    hpl = min(2, heads)
    vrows = dv + ONES_ROWS
    assert (S // TK_ATTN) % 2 == 0
    kern = functools.partial(_attn_kernel, heads=heads, rep=rep, dv=dv, tk=TK_ATTN, heads_per_loop=hpl)
    return pl.pallas_call(
        kern,
        grid=(B, G // gs, S // tq),
        in_specs=[
            pl.BlockSpec((1, heads * DQ_PAD, tq), lambda b, g, i: (b, g, i)),
            pl.BlockSpec((1, gs, S, DQ_PAD), lambda b, g, i: (b, g, 0, 0)),
            pl.BlockSpec((1, gs * vrows, S), lambda b, g, i: (b, g, 0)),
            pl.BlockSpec((1, 1, heads, tq), lambda b, g, i: (b, g, 0, i)),
            pl.BlockSpec((1, 1, gs, S), lambda b, g, i: (b, g, 0, 0)),
        ],
        out_specs=pl.BlockSpec((1, heads * dv, tq), lambda b, g, i: (b, g, i)),
        out_shape=jax.ShapeDtypeStruct((B, G * rep * dv, S), MM_DTYPE),
        scratch_shapes=[pltpu.VMEM((2, hpl, TK_ATTN, tq), jnp.float32),
                        pltpu.VMEM((2, hpl, TK_ATTN, tq), MM_DTYPE)],
        compiler_params=_cparams(("parallel", "parallel", "arbitrary")),
        name=name,
    )(qt, k, vt, qn, kn)


def _post_kernel(x_ref, ln1_ref, wgt_ref, oat_ref, obt_ref, wbrat_ref, wbrbt_ref, woutt_ref, g2_ref,
                 ht_ref, xn2t_ref):
    bf = MM_DTYPE
    x = x_ref[0]
    xn = x * lax.rsqrt(jnp.mean(x * x, axis=-1, keepdims=True) + EPS) * ln1_ref[...]
    xnt = xn.T.astype(bf)
    gates = jax.nn.sigmoid(_dot(wgt_ref[...], xnt))
    ma = _dot(wbrat_ref[...], oat_ref[0])
    mb = _dot(wbrbt_ref[...], obt_ref[0])
    merged = gates[:D_MODEL] * ma + gates[D_MODEL:] * mb
    ht = x.T + _dot(woutt_ref[...], merged.astype(bf))
    ht_ref[0] = ht
    r2 = lax.rsqrt(jnp.mean(ht * ht, axis=0, keepdims=True) + EPS)
    xn2t_ref[0] = (ht * r2 * g2_ref[...]).astype(bf)


def _post(x, oat, obt, W):
    B, S, _ = x.shape
    tm = TM_POST
    const = lambda shape: pl.BlockSpec(shape, lambda b, i: (0,) * len(shape))
    return pl.pallas_call(
        _post_kernel,
        grid=(B, S // tm),
        in_specs=[
            pl.BlockSpec((1, tm, D_MODEL), lambda b, i: (b, i, 0)),
            const((1, D_MODEL)), const((2 * D_MODEL, D_MODEL)),
            pl.BlockSpec((1, H_A * HD_A, tm), lambda b, i: (b, 0, i)),
            pl.BlockSpec((1, H_B * V_B, tm), lambda b, i: (b, 0, i)),
            const((D_MODEL, H_A * HD_A)), const((D_MODEL, H_B * V_B)), const((D_MODEL, D_MODEL)),
            const((D_MODEL, tm)),
        ],
        out_specs=[
            pl.BlockSpec((1, D_MODEL, tm), lambda b, i: (b, 0, i)),
            pl.BlockSpec((1, D_MODEL, tm), lambda b, i: (b, 0, i)),
        ],
        out_shape=[
            jax.ShapeDtypeStruct((B, D_MODEL, S), jnp.float32),
            jax.ShapeDtypeStruct((B, D_MODEL, S), MM_DTYPE),
        ],
        compiler_params=_cparams(("parallel", "parallel")),
        name="post",
    )(x, W["ln1"], W["wgt"], oat, obt, W["wbrat"], W["wbrbt"], W["woutt"], W["g2"])


def _cmpx(xs, i, j):
    a, b = xs[i], xs[j]
    xs[i], xs[j] = jnp.maximum(a, b), jnp.minimum(a, b)


def _bitonic_merge_desc(xs):
    n = len(xs)
    j = n // 2
    while j >= 1:
        for i in range(n):
            l = i ^ j
            if l > i:
                _cmpx(xs, i, l)
        j //= 2


def _bitonic_sort_desc(xs):
    n = len(xs)
    k = 2
    while k <= n:
        j = k // 2
        while j >= 1:
            for i in range(n):
                l = i ^ j
                if l > i:
                    if (i & k) == 0:
                        _cmpx(xs, i, l)
                    else:
                        _cmpx(xs, l, i)
            j //= 2
        k *= 2


_CAND_PAIRS = [(i, j) for i in range(PEER_TOPK) for j in range(PEER_TOPK) if (i + 1) * (j + 1) <= PEER_TOPK]


def _route_kernel(xn2t_ref, wpqt_ref, subk_ref, s_ref, stats_ref, top1_ref):
    bf = MM_DTYPE
    tr = xn2t_ref.shape[2]
    groups = tr // LANES
    qt = _dot(wpqt_ref[...], xn2t_ref[0]).astype(bf)
    row = lax.broadcasted_iota(jnp.int32, (SUBLANES, LANES), 0)
    nblk = N_KEYS // SUBLANES
    for h in range(PEER_HEADS):
        tops = []
        for pp in range(2):
            hp = h * 2 + pp
            st = _dot(subk_ref[hp], qt[hp * HALF_KEY:(hp + 1) * HALF_KEY])
            s_ref[0, hp * N_KEYS:(hp + 1) * N_KEYS, :] = st
            xs = [st[v * SUBLANES:(v + 1) * SUBLANES] for v in range(nblk)]
            _bitonic_sort_desc(xs)
            for shift in (4, 2, 1):
                rolled = [pltpu.roll(xs[nblk - 1 - v], shift, 0) for v in range(nblk)]
                xs = [jnp.maximum(a, b) for a, b in zip(xs, rolled)]
                _bitonic_merge_desc(xs)
            comp = []
            for v in range(PEER_TOPK):
                c = xs[v][:, 0:LANES]
                for gidx in range(1, groups):
                    c = jnp.where(row == gidx, xs[v][:, gidx * LANES:(gidx + 1) * LANES], c)
                comp.append(c)
            tops.append(comp)
        a, b = tops
        cands = [a[i] + b[j] for (i, j) in _CAND_PAIRS]
        cands += [jnp.full((SUBLANES, LANES), -jnp.inf, jnp.float32)] * (64 - len(cands))
        _bitonic_sort_desc(cands)
        cmax = cands[0]
        z = jnp.zeros((SUBLANES, LANES), jnp.float32)
        for kk in range(PEER_TOPK):
            z = z + jnp.exp(cands[kk] - cmax)
        def wide(val):
            return jnp.concatenate([val[gidx:gidx + 1, :] for gidx in range(groups)], axis=1)

        for r, val in enumerate((cands[PEER_TOPK - 1], a[0], b[0], z)):
            stats_ref[0, h * 4 + r:h * 4 + r + 1, :] = wide(val)
        for v in range(PEER_TOPK):
            top1_ref[0, h * PEER_TOPK + v:h * PEER_TOPK + v + 1, :] = wide(a[v])


def _route(xn2t, W):
    B, _, S = xn2t.shape
    tr = TR_ROUTE
    const = lambda shape: pl.BlockSpec(shape, lambda b, i: (0,) * len(shape))
    return pl.pallas_call(
        _route_kernel,
        grid=(B, S // tr),
        in_specs=[
            pl.BlockSpec((1, D_MODEL, tr), lambda b, i: (b, 0, i)),
            const((PEER_HEADS * D_KEY, D_MODEL)),
            const((PEER_HEADS * 2, N_KEYS, HALF_KEY)),
        ],
        out_specs=[
            pl.BlockSpec((1, PEER_HEADS * 2 * N_KEYS, tr), lambda b, i: (b, 0, i)),
            pl.BlockSpec((1, PEER_HEADS * 4, tr), lambda b, i: (b, 0, i)),
            pl.BlockSpec((1, PEER_HEADS * PEER_TOPK, tr), lambda b, i: (b, 0, i)),
        ],
        out_shape=[
            jax.ShapeDtypeStruct((B, PEER_HEADS * 2 * N_KEYS, S), jnp.float32),
            jax.ShapeDtypeStruct((B, PEER_HEADS * 4, S), jnp.float32),
            jax.ShapeDtypeStruct((B, PEER_HEADS * PEER_TOPK, S), jnp.float32),
        ],
        compiler_params=_cparams(("parallel", "parallel")),
        name="route",
    )(xn2t, W["wpqt"], W["subk"])


def _peer_kernel(xn2t_ref, s_ref, stats_ref, top1_ref, ht_ref, u_ref, vt_ref, gfin_ref, y_ref,
                 e1_ref, e2_ref, acc_ref, hid_ref, a_ref, s1_ref, thr_ref):
    e = pl.program_id(2)
    ne = pl.num_programs(2)
    ec = u_ref.shape[0]
    tt = xn2t_ref.shape[2]
    rows_per_chunk = ec // N_KEYS

    @pl.when(e == 0)
    def _():
        for h in range(PEER_HEADS):
            amax = stats_ref[0, h * 4 + 1:h * 4 + 2, :]
            bmax = stats_ref[0, h * 4 + 2:h * 4 + 3, :]
            z = stats_ref[0, h * 4 + 3:h * 4 + 4, :]
            s1 = s_ref[0, (2 * h) * N_KEYS:(2 * h + 1) * N_KEYS, :]
            s2 = s_ref[0, (2 * h + 1) * N_KEYS:(2 * h + 2) * N_KEYS, :]
            e1_ref[h * N_KEYS:(h + 1) * N_KEYS, :] = jnp.exp(s1 - amax)
            e2_ref[h * N_KEYS:(h + 1) * N_KEYS, :] = jnp.exp(s2 - bmax) / z
            s1_ref[h * N_KEYS:(h + 1) * N_KEYS, :] = s1
            tau = stats_ref[0, h * 4:h * 4 + 1, :]
            thr = jnp.full((N_KEYS, tt), jnp.inf, jnp.float32)
            for r in range(PEER_TOPK // 2):
                ar = top1_ref[0, h * PEER_TOPK + r:h * PEER_TOPK + r + 1, :]
                thr = jnp.where(ar + s2 >= tau, ar, thr)
            low = jnp.full((1, tt), jnp.inf, jnp.float32)
            for r in range(PEER_TOPK // 2, PEER_TOPK):
                ar = top1_ref[0, h * PEER_TOPK + r:h * PEER_TOPK + r + 1, :]
                low = jnp.where(ar + bmax >= tau, ar, low)
            thr = jnp.where(s2 >= bmax, jnp.minimum(thr, low), thr)
            thr_ref[h * N_KEYS:(h + 1) * N_KEYS, :] = thr
        acc_ref[...] = jnp.zeros_like(acc_ref)

    def build(rs):
        first = [e * rows_per_chunk + r for r in rs]
        s1rows = [[s1_ref[pl.ds(h * N_KEYS + i, 1), :] for h in range(PEER_HEADS)] for i in first]
        e1rows = [[e1_ref[pl.ds(h * N_KEYS + i, 1), :] for h in range(PEER_HEADS)] for i in first]
        for t0 in range(0, tt, LANES):
            lanes = slice(t0, t0 + LANES)
            for j0 in range(0, N_KEYS, PEER_TJ):
                ws = [jnp.zeros((PEER_TJ, LANES), jnp.float32) for _ in rs]
                for h in range(PEER_HEADS):
                    thr = thr_ref[h * N_KEYS + j0:h * N_KEYS + j0 + PEER_TJ, lanes]
                    e2 = e2_ref[h * N_KEYS + j0:h * N_KEYS + j0 + PEER_TJ, lanes]
                    for k in range(len(rs)):
                        ws[k] = ws[k] + jnp.where(s1rows[k][h][:, lanes] >= thr,
                                                  e1rows[k][h][:, lanes] * e2, 0.0)
                for k, r in enumerate(rs):
                    rows = slice(r * N_KEYS + j0, r * N_KEYS + j0 + PEER_TJ)
                    hid = hid_ref[rows, lanes]
                    act = 0.5 * hid * (1.0 + lax.erf(hid * (2.0 ** -0.5)))
                    a_ref[rows, lanes] = (act * ws[k]).astype(MM_DTYPE)

    part = ec // PEER_PARTS
    rows_part = rows_per_chunk // PEER_PARTS
    xt = xn2t_ref[0]
    for k in range(PEER_PARTS):
        hid_ref[k * part:(k + 1) * part] = _dot(u_ref[k * part:(k + 1) * part], xt)
    for k in range(PEER_PARTS):
        build(list(range(k * rows_part, (k + 1) * rows_part)))
        acc_ref[...] += _dot(vt_ref[:, k * part:(k + 1) * part], a_ref[k * part:(k + 1) * part])

    @pl.when(e == ne - 1)
    def _():
        yt = ht_ref[0] + acc_ref[...]
        r = lax.rsqrt(jnp.mean(yt * yt, axis=0, keepdims=True) + EPS)
        y_ref[0] = (yt * r * gfin_ref[...]).T


def _peer(xn2t, s, stats, top1, ht, W):
    B, _, S = xn2t.shape
    tt, ec = TT_PEER, EC_PEER
    return pl.pallas_call(
        _peer_kernel,
        grid=(B, S // tt, N_EXPERTS // ec),
        in_specs=[
            pl.BlockSpec((1, D_MODEL, tt), lambda b, i, e: (b, 0, i)),
            pl.BlockSpec((1, PEER_HEADS * 2 * N_KEYS, tt), lambda b, i, e: (b, 0, i)),
            pl.BlockSpec((1, PEER_HEADS * 4, tt), lambda b, i, e: (b, 0, i)),
            pl.BlockSpec((1, PEER_HEADS * PEER_TOPK, tt), lambda b, i, e: (b, 0, i)),
            pl.BlockSpec((1, D_MODEL, tt), lambda b, i, e: (b, 0, i)),
            pl.BlockSpec((ec, D_MODEL), lambda b, i, e: (e, 0)),
            pl.BlockSpec((D_MODEL, ec), lambda b, i, e: (0, e)),
            pl.BlockSpec((D_MODEL, tt), lambda b, i, e: (0, 0)),
        ],
        out_specs=pl.BlockSpec((1, tt, D_MODEL), lambda b, i, e: (b, i, 0)),
        out_shape=jax.ShapeDtypeStruct((B, S, D_MODEL), jnp.float32),
        scratch_shapes=[
            pltpu.VMEM((PEER_HEADS * N_KEYS, tt), jnp.float32),
            pltpu.VMEM((PEER_HEADS * N_KEYS, tt), jnp.float32),
            pltpu.VMEM((D_MODEL, tt), jnp.float32),
            pltpu.VMEM((ec, tt), jnp.float32),
            pltpu.VMEM((ec, tt), MM_DTYPE),
            pltpu.VMEM((PEER_HEADS * N_KEYS, tt), jnp.float32),
            pltpu.VMEM((PEER_HEADS * N_KEYS, tt), jnp.float32),
        ],
        compiler_params=_cparams(("parallel", "parallel", "arbitrary")),
        name="peer",
    )(xn2t, s, stats, top1, ht, W["u"], W["vt"], W["gfin"])


def _forward(x, W):
    qat, ka, vat, qan, kan, qbt, kb, vbt, qbn, kbn = _proj(x, W)
    oat = _attention(qat, ka, vat, qan, kan, groups_per_step=1, rep=H_A // G_A, dv=HD_A, name="attn_gqa")
    obt = _attention(qbt, kb, vbt, qbn, kbn, groups_per_step=2, rep=1, dv=V_B, name="attn_mla")
    ht, xn2t = _post(x, oat, obt, W)
    s, stats, top1 = _route(xn2t, W)
    return _peer(xn2t, s, stats, top1, ht, W)


def kernel(x_prompt, x_sample, ln1_g, w_in, qa_norm_g, ka_norm_g, cq_norm_g, ckv_norm_g, w_uq, w_ukv,
           w_br_a, w_br_b, w_out, ln2_g, w_pq, sub_keys, expert_u, expert_v, final_g):
    params = (ln1_g, w_in, qa_norm_g, ka_norm_g, cq_norm_g, ckv_norm_g, w_uq, w_ukv,
              w_br_a, w_br_b, w_out, ln2_g, w_pq, sub_keys, expert_u, expert_v, final_g)
    assert x_prompt.shape[1] == x_sample.shape[1]
    W = _prepare(x_prompt.shape[1], *params)
    return (_forward(x_prompt, W), _forward(x_sample, W))
```
